```python
import math, functools
import jax, jax.numpy as jnp
from jax import lax
import numpy as np

D_MODEL = 1024
BATCH = 4
SEQ = 4096
DEPTH = 2
DEC_BATCH = 128
DEC_SEQ = 4
PAST_LEN = 16384
PAGE_SIZE = 128

N_MIXERS = 2
N_HEADS = 16
Q_LORA = 384
KV_LORA = 256
QK_NOPE = 64
QK_ROPE = 32
V_HEAD = 64
QK_HEAD = QK_NOPE + QK_ROPE
ROPE_THETA = 10000.0
ATTN_SCALE = QK_HEAD ** -0.5
Q_BLOCK = 128
CONV_W = 3
D_FF = 2816
RMS_EPS = 1e-6
LN_EPS = 1e-5
DEEPNORM_ALPHA = (2 * DEPTH) ** 0.25
DEEPNORM_BETA = (8 * DEPTH) ** -0.25

kernel_name = "mla_shortconv_convffn_deepnorm_step"


def rms_norm(x, g):
    xf = x.astype(jnp.float32)
    y = xf * lax.rsqrt(jnp.mean(xf * xf, axis=-1, keepdims=True) + RMS_EPS)
    return (y * g.astype(jnp.float32)).astype(x.dtype)


def layer_norm(x, g, b):
    xf = x.astype(jnp.float32)
    mu = jnp.mean(xf, axis=-1, keepdims=True)
    xc = xf - mu
    var = jnp.mean(xc * xc, axis=-1, keepdims=True)
    y = xc * lax.rsqrt(var + LN_EPS) * g.astype(jnp.float32) + b.astype(jnp.float32)
    return y.astype(x.dtype)


def rope_cos_sin(pos):
    inv = 1.0 / (ROPE_THETA ** (jnp.arange(0, QK_ROPE, 2, dtype=jnp.float32) / QK_ROPE))
    ang = pos.astype(jnp.float32)[:, None] * inv[None, :]
    return jnp.cos(ang), jnp.sin(ang)


def apply_rope(x, cos, sin):
    xf = x.astype(jnp.float32)
    x1, x2 = xf[..., : QK_ROPE // 2], xf[..., QK_ROPE // 2 :]
    return jnp.concatenate([x1 * cos - x2 * sin, x1 * sin + x2 * cos], axis=-1).astype(x.dtype)


def causal_dwconv(u, prev, w):
    t = u.shape[1]
    up = jnp.concatenate([prev, u], axis=1)
    y = w[0] * up[:, 0:t]
    for j in range(1, CONV_W):
        y = y + w[j] * up[:, j:j + t]
    return y, up[:, t:]


def mla_project(x, pos, w_a_down, g_q_norm, g_kv_norm, w_a_uq):
    b, t, _ = x.shape
    d = x @ w_a_down
    c_q = rms_norm(d[..., :Q_LORA], g_q_norm)
    c_kv = rms_norm(d[..., Q_LORA:Q_LORA + KV_LORA], g_kv_norm)
    k_r = d[..., Q_LORA + KV_LORA:]
    q = (c_q @ w_a_uq).reshape(b, t, N_HEADS, QK_HEAD)
    cos, sin = rope_cos_sin(pos)
    q_nope = q[..., :QK_NOPE]
    q_rope = apply_rope(q[..., QK_NOPE:], cos[:, None, :], sin[:, None, :])
    k_r = apply_rope(k_r, cos, sin)
    return q_nope, q_rope, c_kv, k_r


def mla_attend_prompt(q_nope, q_rope, c_kv, k_r, w_a_uk, w_a_uv):
    b, s = q_nope.shape[0], q_nope.shape[1]
    k_nope = jnp.einsum('bsc,chd->bshd', c_kv, w_a_uk)
    v = jnp.einsum('bsc,chd->bshd', c_kv, w_a_uv)
    n_blk = s // Q_BLOCK
    qn = q_nope.reshape(b, n_blk, Q_BLOCK, N_HEADS, QK_NOPE).transpose(1, 0, 2, 3, 4)
    qr = q_rope.reshape(b, n_blk, Q_BLOCK, N_HEADS, QK_ROPE).transpose(1, 0, 2, 3, 4)
    k_pos = jnp.arange(s)

    def block(args):
        qn_b, qr_b, i = args
        sc = (jnp.einsum('bqhd,bkhd->bhqk', qn_b, k_nope)
              + jnp.einsum('bqhd,bkd->bhqk', qr_b, k_r)).astype(jnp.float32) * ATTN_SCALE
        q_pos = i * Q_BLOCK + jnp.arange(Q_BLOCK)
        sc = jnp.where(k_pos[None, :] <= q_pos[:, None], sc, -jnp.inf)
        p = jax.nn.softmax(sc, axis=-1).astype(v.dtype)
        return jnp.einsum('bhqk,bkhd->bqhd', p, v)

    o = lax.map(block, (qn, qr, jnp.arange(n_blk)))
    return o.transpose(1, 0, 2, 3, 4).reshape(b, s, N_HEADS, V_HEAD)


def mla_attend_sample(q_nope, q_rope, c_kv, k_r, cache_kv_latent, cache_k_rope, page_table, w_a_uk, w_a_uv):
    b, t = q_nope.shape[0], q_nope.shape[1]
    past_c = cache_kv_latent[page_table].reshape(b, -1, KV_LORA)
    past_r = cache_k_rope[page_table].reshape(b, -1, QK_ROPE)
    q_lat = jnp.einsum('bqhd,chd->bqhc', q_nope, w_a_uk)
    sc_past = (jnp.einsum('bqhc,bkc->bhqk', q_lat, past_c)
               + jnp.einsum('bqhd,bkd->bhqk', q_rope, past_r)).astype(jnp.float32) * ATTN_SCALE
    sc_new = (jnp.einsum('bqhc,bkc->bhqk', q_lat, c_kv)
              + jnp.einsum('bqhd,bkd->bhqk', q_rope, k_r)).astype(jnp.float32) * ATTN_SCALE
    causal = jnp.arange(t)[None, :] <= jnp.arange(t)[:, None]
    sc_new = jnp.where(causal, sc_new, -jnp.inf)
    p = jax.nn.softmax(jnp.concatenate([sc_past, sc_new], axis=-1), axis=-1).astype(c_kv.dtype)
    n_past = past_c.shape[1]
    o_lat = (jnp.einsum('bhqk,bkc->bqhc', p[..., :n_past], past_c)
             + jnp.einsum('bhqk,bkc->bqhc', p[..., n_past:], c_kv))
    return jnp.einsum('bqhc,chd->bqhd', o_lat, w_a_uv)


def short_conv_mixer(x, prev, w_b_in, w_b_conv, w_b_out):
    bch = x @ w_b_in
    g_b = bch[..., :D_MODEL]
    g_c = bch[..., D_MODEL:2 * D_MODEL]
    h = bch[..., 2 * D_MODEL:]
    y, new_state = causal_dwconv(g_c * h, prev, w_b_conv)
    return (g_b * y) @ w_b_out, new_state


def conv_ffn(x, prev, w_up, w_conv, b_conv, w_down):
    h = x @ w_up
    h, new_state = causal_dwconv(h, prev, w_conv)
    h = h + b_conv
    a, g = h[..., :D_FF], h[..., D_FF:]
    return (jax.nn.silu(g) * a) @ w_down, new_state


def trunk(x, pos, attend, conv_prev, ffn_prev, w):
    b, t, _ = x.shape
    ffn_states = []
    latent_rows = rope_rows = conv_state = None
    for i in range(DEPTH):
        if i % N_MIXERS == 0:
            q_nope, q_rope, c_kv, k_r = mla_project(x, pos, w['w_a_down'], w['g_q_norm'], w['g_kv_norm'], w['w_a_uq'])
            o = attend(q_nope, q_rope, c_kv, k_r)
            mix = o.reshape(b, t, N_HEADS * V_HEAD) @ w['w_a_out']
            latent_rows, rope_rows = c_kv, k_r
        else:
            mix, conv_state = short_conv_mixer(x, conv_prev, w['w_b_in'], w['w_b_conv'], w['w_b_out'])
        x = layer_norm(DEEPNORM_ALPHA * x + mix, w['ln_g'][i, 0], w['ln_b'][i, 0])
        f, fs = conv_ffn(x, ffn_prev[i], w['w_f_up'][i], w['w_f_conv'][i], w['b_f_conv'][i], w['w_f_down'][i])
        ffn_states.append(fs)
        x = layer_norm(DEEPNORM_ALPHA * x + f, w['ln_g'][i, 1], w['ln_b'][i, 1])
    return x, latent_rows, rope_rows, conv_state, jnp.stack(ffn_states)


def setup_inputs(seed: int = 0) -> dict:
    key = jax.random.key(seed)
    ks = jax.random.split(key, 24)
    n_pages = PAST_LEN // PAGE_SIZE
    n_used = DEC_BATCH * n_pages
    n_pool = (n_used * 5) // 4
    nrm = jax.random.normal
    f32 = jnp.float32
    page_table = jax.random.permutation(ks[6], n_pool)[:n_used].reshape(DEC_BATCH, n_pages).astype(jnp.int32)
    return {
        'x_prompt': nrm(ks[0], (BATCH, SEQ, D_MODEL), f32),
        'x_sample': nrm(ks[1], (DEC_BATCH, DEC_SEQ, D_MODEL), f32),
        'cache_kv_latent': nrm(ks[2], (n_pool, PAGE_SIZE, KV_LORA), f32),
        'cache_k_rope': nrm(ks[3], (n_pool, PAGE_SIZE, QK_ROPE), f32),
        'state_conv_b': nrm(ks[4], (DEC_BATCH, CONV_W - 1, D_MODEL), f32),
        'state_ffn_conv': nrm(ks[5], (DEPTH, DEC_BATCH, CONV_W - 1, 2 * D_FF), f32),
        'page_table': page_table,
        'w_a_down': nrm(ks[7], (D_MODEL, Q_LORA + KV_LORA + QK_ROPE), f32) * D_MODEL ** -0.5,
        'g_q_norm': 1.0 + 0.01 * nrm(ks[8], (Q_LORA,), f32),
        'g_kv_norm': 1.0 + 0.01 * nrm(ks[9], (KV_LORA,), f32),
        'w_a_uq': nrm(ks[10], (Q_LORA, N_HEADS * QK_HEAD), f32) * Q_LORA ** -0.5,
        'w_a_uk': nrm(ks[11], (KV_LORA, N_HEADS, QK_NOPE), f32) * KV_LORA ** -0.5,
        'w_a_uv': nrm(ks[12], (KV_LORA, N_HEADS, V_HEAD), f32) * (KV_LORA ** -0.5 * DEEPNORM_BETA),
        'w_a_out': nrm(ks[13], (N_HEADS * V_HEAD, D_MODEL), f32) * ((N_HEADS * V_HEAD) ** -0.5 * DEEPNORM_BETA),
        'w_b_in': nrm(ks[14], (D_MODEL, 3 * D_MODEL), f32) * D_MODEL ** -0.5,
        'w_b_conv': nrm(ks[15], (CONV_W, D_MODEL), f32) * CONV_W ** -0.5,
        'w_b_out': nrm(ks[16], (D_MODEL, D_MODEL), f32) * (D_MODEL ** -0.5 * DEEPNORM_BETA),
        'w_f_up': nrm(ks[17], (DEPTH, D_MODEL, 2 * D_FF), f32) * D_MODEL ** -0.5,
        'w_f_conv': nrm(ks[18], (DEPTH, CONV_W, 2 * D_FF), f32) * CONV_W ** -0.5,
        'b_f_conv': 0.01 * nrm(ks[19], (DEPTH, 2 * D_FF), f32),
        'w_f_down': nrm(ks[20], (DEPTH, D_FF, D_MODEL), f32) * (D_FF ** -0.5 * DEEPNORM_BETA),
        'ln_g': 1.0 + 0.01 * nrm(ks[21], (DEPTH, 2, D_MODEL), f32),
        'ln_b': 0.01 * nrm(ks[22], (DEPTH, 2, D_MODEL), f32),
    }


def reference(x_prompt, x_sample, cache_kv_latent, cache_k_rope, state_conv_b, state_ffn_conv, page_table,
              w_a_down, g_q_norm, g_kv_norm, w_a_uq, w_a_uk, w_a_uv, w_a_out,
              w_b_in, w_b_conv, w_b_out, w_f_up, w_f_conv, b_f_conv, w_f_down, ln_g, ln_b):
    w = dict(w_a_down=w_a_down, g_q_norm=g_q_norm, g_kv_norm=g_kv_norm, w_a_uq=w_a_uq,
             w_a_out=w_a_out, w_b_in=w_b_in, w_b_conv=w_b_conv, w_b_out=w_b_out,
             w_f_up=w_f_up, w_f_conv=w_f_conv, b_f_conv=b_f_conv, w_f_down=w_f_down,
             ln_g=ln_g, ln_b=ln_b)
    b_p, s_p, _ = x_prompt.shape
    b_s, s_s, _ = x_sample.shape
    past_len = page_table.shape[1] * cache_kv_latent.shape[1]

    attend_p = functools.partial(mla_attend_prompt, w_a_uk=w_a_uk, w_a_uv=w_a_uv)
    conv0 = jnp.zeros((b_p, CONV_W - 1, D_MODEL), x_prompt.dtype)
    ffn0 = jnp.zeros((DEPTH, b_p, CONV_W - 1, 2 * D_FF), x_prompt.dtype)
    y_prompt, lat_p, rope_p, conv_b_prompt, ffn_conv_prompt = trunk(
        x_prompt, jnp.arange(s_p), attend_p, conv0, ffn0, w)
    kv_latent_prompt = lat_p.reshape(b_p, s_p // PAGE_SIZE, PAGE_SIZE, KV_LORA)
    k_rope_prompt = rope_p.reshape(b_p, s_p // PAGE_SIZE, PAGE_SIZE, QK_ROPE)

    attend_s = functools.partial(mla_attend_sample, cache_kv_latent=cache_kv_latent, cache_k_rope=cache_k_rope,
                                 page_table=page_table, w_a_uk=w_a_uk, w_a_uv=w_a_uv)
    y_sample, kv_latent_sample, k_rope_sample, conv_b_sample, ffn_conv_sample = trunk(
        x_sample, past_len + jnp.arange(s_s), attend_s, state_conv_b, state_ffn_conv, w)

    return (y_prompt, y_sample, kv_latent_prompt, k_rope_prompt, kv_latent_sample, k_rope_sample,
            conv_b_prompt, conv_b_sample, ffn_conv_prompt, ffn_conv_sample)
```

```python
import functools
import math

import jax
import jax.numpy as jnp
from jax import lax
from jax.experimental import pallas as pl
from jax.experimental.pallas import tpu as pltpu

D_MODEL = 1024
N_HEADS = 16
Q_LORA = 384
KV_LORA = 256
QK_NOPE = 64
QK_ROPE = 32
HALF_ROPE = QK_ROPE // 2
V_HEAD = 64
QK_HEAD = QK_NOPE + QK_ROPE
ROPE_THETA = 10000.0
ATTN_SCALE = QK_HEAD ** -0.5
CONV_W = 3
D_FF = 2816
RMS_EPS = 1e-6
LN_EPS = 1e-5
DEPTH = 2
DEEPNORM_ALPHA = (2 * DEPTH) ** 0.25
PAGE_SIZE = 128

LANES = 128
SUBLANES = 8
HEAD_PAD = LANES
VMEM_LIMIT = 56 * 1024 * 1024

TM = 512
TQ = 512
TK = 512
FF_CHUNK = 1408
N_FF_CHUNKS = D_FF // FF_CHUNK
PAGES_PER_STEP = 16
HALO_PROMPT = SUBLANES

BF16 = jnp.bfloat16
F32 = jnp.float32


def _dot(a, b):
    return jnp.dot(a, b, preferred_element_type=F32)


def _dot_nt(a, b):
    return lax.dot_general(a, b, (((1,), (1,)), ((), ())), preferred_element_type=F32)


def _layer_norm(v, g, b):
    mu = jnp.mean(v, axis=-1, keepdims=True)
    vc = v - mu
    var = jnp.mean(vc * vc, axis=-1, keepdims=True)
    return vc * lax.rsqrt(var + LN_EPS) * g + b


def _rms_norm(v, g):
    return v * lax.rsqrt(jnp.mean(v * v, axis=-1, keepdims=True) + RMS_EPS) * g


def _cparams(sem):
    return pltpu.CompilerParams(dimension_semantics=sem, vmem_limit_bytes=VMEM_LIMIT)


def _mla_proj_kernel(emit_kv, x_ref, wd_ref, gq_ref, gkv_ref, wq_ref, cosg_ref, sing_ref,
                     ckr_ref, skr_ref, *rest):
    if emit_kv:
        wk_ref, e_ref, wv_ref, q_ref, ckv_ref, kr_ref, k_ref, v_ref = rest
    else:
        q_ref, ckv_ref, kr_ref = rest
    xb = x_ref[...].astype(BF16)
    d = _dot(xb, wd_ref[...])
    c_q = _rms_norm(d[:, :Q_LORA], gq_ref[...])
    c_kv = _rms_norm(d[:, Q_LORA:Q_LORA + KV_LORA], gkv_ref[...])
    ckv_ref[...] = c_kv
    o = Q_LORA + KV_LORA
    krg = d[:, o:o + LANES] * ckr_ref[...] + d[:, o + LANES:o + 2 * LANES] * skr_ref[...]
    kr_ref[...] = krg[:, :QK_ROPE]
    q2 = _dot(c_q.astype(BF16), wq_ref[...])
    cosg = cosg_ref[...] * ATTN_SCALE
    sing = sing_ref[...] * ATTN_SCALE
    qw = N_HEADS * HEAD_PAD
    for h in range(N_HEADS):
        lo = h * HEAD_PAD
        q_ref[:, lo:lo + HEAD_PAD] = (q2[:, lo:lo + HEAD_PAD] * cosg
                                      + q2[:, qw + lo:qw + lo + HEAD_PAD] * sing).astype(BF16)
    if emit_kv:
        ckvb = c_kv.astype(BF16)
        k = _dot(ckvb, wk_ref[...]) + _dot(krg.astype(BF16), e_ref[...])
        k_ref[...] = k.astype(BF16)
        v_ref[...] = _dot(ckvb, wv_ref[...]).astype(BF16)


def _mla_proj(x, tabs, w, emit_kv, pos_tiles):
    t = x.shape[0]
    n = t // TM
    qw = N_HEADS * HEAD_PAD
    const = lambda i: (0, 0)
    row = lambda i: (i, 0)
    tab = lambda i: (i % pos_tiles, 0)
    in_specs = [
        pl.BlockSpec((TM, D_MODEL), row),
        pl.BlockSpec(w['wd'].shape, const),
        pl.BlockSpec((1, Q_LORA), const),
        pl.BlockSpec((1, KV_LORA), const),
        pl.BlockSpec(w['wq'].shape, const),
    ] + [pl.BlockSpec((TM, LANES), tab)] * 4
    args = [x, w['wd'], w['gq'], w['gkv'], w['wq']] + list(tabs)
    out_shape = [jax.ShapeDtypeStruct((t, qw), BF16),
                 jax.ShapeDtypeStruct((t, KV_LORA), F32),
                 jax.ShapeDtypeStruct((t, QK_ROPE), F32)]
    out_specs = [pl.BlockSpec((TM, qw), row), pl.BlockSpec((TM, KV_LORA), row),
                 pl.BlockSpec((TM, QK_ROPE), row)]
    if emit_kv:
        in_specs += [pl.BlockSpec(w['wk'].shape, const), pl.BlockSpec(w['e'].shape, const),
                     pl.BlockSpec(w['wv'].shape, const)]
        args += [w['wk'], w['e'], w['wv']]
        out_shape += [jax.ShapeDtypeStruct((t, qw), BF16),
                      jax.ShapeDtypeStruct((t, N_HEADS * V_HEAD), BF16)]
        out_specs += [pl.BlockSpec((TM, qw), row), pl.BlockSpec((TM, N_HEADS * V_HEAD), row)]
    return pl.pallas_call(
        functools.partial(_mla_proj_kernel, emit_kv),
        grid=(n,), in_specs=in_specs, out_specs=out_specs, out_shape=out_shape,
        compiler_params=_cparams(("parallel",)), name="mla_proj_kv" if emit_kv else "mla_proj",
    )(*args)


def _flash_kernel(qi_tab, ki_tab, q_ref, k_ref, v_ref, o_ref, m_sc, l_sc, acc_sc):
    step = pl.program_id(2)
    qi = qi_tab[step]
    ki = ki_tab[step]

    @pl.when(ki == 0)
    def _():
        m_sc[...] = jnp.full(m_sc.shape, -jnp.inf, F32)
        l_sc[...] = jnp.zeros(l_sc.shape, F32)
        acc_sc[...] = jnp.zeros(acc_sc.shape, F32)

    row = lax.broadcasted_iota(jnp.int32, (TQ, TK), 0)
    col = lax.broadcasted_iota(jnp.int32, (TQ, TK), 1)
    visible = col <= row + (qi - ki) * TQ
    v = v_ref[...]
    for hh in range(2):
        q = q_ref[:, hh * HEAD_PAD:(hh + 1) * HEAD_PAD]
        k = k_ref[:, hh * HEAD_PAD:(hh + 1) * HEAD_PAD]
        s = jnp.where(visible, _dot_nt(q, k), -jnp.inf)
        m_prev = m_sc[hh]
        m_new = jnp.maximum(m_prev, jnp.max(s, axis=-1, keepdims=True))
        alpha = jnp.exp(m_prev - m_new)
        p = jnp.exp(s - m_new)
        l_sc[hh] = alpha * l_sc[hh] + jnp.sum(p, axis=-1, keepdims=True)
        acc_sc[hh] = alpha * acc_sc[hh] + _dot(p.astype(BF16), v)
        m_sc[hh] = m_new

    @pl.when(ki == qi)
    def _():
        lane = lax.broadcasted_iota(jnp.int32, (TQ, 2 * V_HEAD), 1)
        o0 = acc_sc[0] / l_sc[0]
        o1 = acc_sc[1] / l_sc[1]
        o_ref[...] = jnp.where(lane < V_HEAD, o0, o1).astype(BF16)


def _flash_attention(q, k, v, batch, seq):
    nq = seq // TQ
    pairs = [(a, b) for a in range(nq) for b in range(a + 1)]
    qi_tab = jnp.asarray([p[0] for p in pairs], jnp.int32)
    ki_tab = jnp.asarray([p[1] for p in pairs], jnp.int32)
    grid_spec = pltpu.PrefetchScalarGridSpec(
        num_scalar_prefetch=2,
        grid=(batch, N_HEADS // 2, len(pairs)),
        in_specs=[
            pl.BlockSpec((TQ, 2 * HEAD_PAD), lambda b, h, s, qt, kt: (b * nq + qt[s], h)),
            pl.BlockSpec((TK, 2 * HEAD_PAD), lambda b, h, s, qt, kt: (b * nq + kt[s], h)),
            pl.BlockSpec((TK, 2 * V_HEAD), lambda b, h, s, qt, kt: (b * nq + kt[s], h)),
        ],
        out_specs=pl.BlockSpec((TQ, 2 * V_HEAD), lambda b, h, s, qt, kt: (b * nq + qt[s], h)),
        scratch_shapes=[pltpu.VMEM((2, TQ, 1), F32), pltpu.VMEM((2, TQ, 1), F32),
                        pltpu.VMEM((2, TQ, 2 * V_HEAD), F32)],
    )
    return pl.pallas_call(
        _flash_kernel, grid_spec=grid_spec,
        out_shape=jax.ShapeDtypeStruct((batch * seq, N_HEADS * V_HEAD), BF16),
        compiler_params=_cparams(("parallel", "parallel", "arbitrary")), name="flash_attn",
    )(qi_tab, ki_tab, q, k, v)


def _absorb_kernel(q_ref, w_ref, o_ref):
    o_ref[...] = _dot(q_ref[...], w_ref[...]).astype(BF16)


def _absorb_q(q, w_ukt):
    t = q.shape[0]
    return pl.pallas_call(
        _absorb_kernel, grid=(N_HEADS,),
        in_specs=[pl.BlockSpec((t, HEAD_PAD), lambda h: (0, h)),
                  pl.BlockSpec((None, HEAD_PAD, KV_LORA), lambda h: (h, 0, 0))],
        out_specs=pl.BlockSpec((None, t, KV_LORA), lambda h: (h, 0, 0)),
        out_shape=jax.ShapeDtypeStruct((N_HEADS, t, KV_LORA), BF16),
        compiler_params=_cparams(("parallel",)), name="absorb_q",
    )(q, w_ukt)


def _unabsorb_kernel(o_ref, w_ref, out_ref):
    out_ref[...] = (_dot(o_ref[0], w_ref[0]) + _dot(o_ref[1], w_ref[1])).astype(BF16)


def _unabsorb_o(o_lat, w_uv_pair):
    t = o_lat.shape[1]
    return pl.pallas_call(
        _unabsorb_kernel, grid=(N_HEADS // 2,),
        in_specs=[pl.BlockSpec((2, t, KV_LORA), lambda j: (j, 0, 0)),
                  pl.BlockSpec((2, KV_LORA, 2 * V_HEAD), lambda j: (j, 0, 0))],
        out_specs=pl.BlockSpec((t, 2 * V_HEAD), lambda j: (0, j)),
        out_shape=jax.ShapeDtypeStruct((t, N_HEADS * V_HEAD), BF16),
        compiler_params=_cparams(("parallel",)), name="unabsorb_o",
    )(o_lat, w_uv_pair)


def _decode_kernel(n_new, pt_ref, ql_ref, qr_ref, cn_ref, rn_ref, *rest):
    c_refs = rest[:PAGES_PER_STEP]
    r_refs = rest[PAGES_PER_STEP:2 * PAGES_PER_STEP]
    o_ref, m_sc, l_sc, acc_sc = rest[2 * PAGES_PER_STEP:]
    g = pl.program_id(1)
    rows = ql_ref.shape[0]

    @pl.when(g == 0)
    def _():
        m_sc[...] = jnp.full(m_sc.shape, -jnp.inf, F32)
        l_sc[...] = jnp.zeros(l_sc.shape, F32)
        acc_sc[...] = jnp.zeros(acc_sc.shape, F32)

    ql = ql_ref[...]
    qr = qr_ref[...]
    cs = [c_refs[j][...].astype(BF16) for j in range(PAGES_PER_STEP)]
    ss = [_dot_nt(ql, cs[j]) + _dot_nt(qr, r_refs[j][...].astype(BF16))
          for j in range(PAGES_PER_STEP)]
    mx = ss[0]
    for j in range(1, PAGES_PER_STEP):
        mx = jnp.maximum(mx, ss[j])
    m_prev = m_sc[...]
    m_new = jnp.maximum(m_prev, jnp.max(mx, axis=-1, keepdims=True))
    alpha = jnp.exp(m_prev - m_new)
    psum = jnp.zeros((rows, PAGE_SIZE), F32)
    pv = jnp.zeros((rows, KV_LORA), F32)
    for j in range(PAGES_PER_STEP):
        p = jnp.exp(ss[j] - m_new)
        psum = psum + p
        pv = pv + _dot(p.astype(BF16), cs[j])
    l_sc[...] = alpha * l_sc[...] + jnp.sum(psum, axis=-1, keepdims=True)
    acc_sc[...] = alpha * acc_sc[...] + pv
    m_sc[...] = m_new

    @pl.when(g == pl.num_programs(1) - 1)
    def _():
        qlf = ql.astype(F32)
        qrf = qr.astype(F32)
        row = lax.broadcasted_iota(jnp.int32, (rows, 1), 0)
        s_new = []
        for j in range(n_new):
            sj = (jnp.sum(qlf * cn_ref[j:j + 1, :], axis=-1, keepdims=True)
                  + jnp.sum(qrf * rn_ref[j:j + 1, :], axis=-1, keepdims=True))
            s_new.append(jnp.where(row >= j * N_HEADS, sj, -jnp.inf))
        m_old = m_sc[...]
        m_fin = m_old
        for sj in s_new:
            m_fin = jnp.maximum(m_fin, sj)
        a = jnp.exp(m_old - m_fin)
        l_fin = a * l_sc[...]
        acc = a * acc_sc[...]
        for j in range(n_new):
            pj = jnp.exp(s_new[j] - m_fin)
            l_fin = l_fin + pj
            acc = acc + pj * cn_ref[j:j + 1, :]
        o_ref[...] = acc / l_fin


def _decode_attention(q_lat, q_rope, c_new, r_new, cache_c, cache_r, page_table, n_new):
    b, rows, _ = q_lat.shape
    n_pages = page_table.shape[1]
    n_groups = n_pages // PAGES_PER_STEP

    def page_spec(width, j):
        return pl.BlockSpec((None, PAGE_SIZE, width),
                            lambda i, g, pt: (pt[i, g * PAGES_PER_STEP + j], 0, 0))

    seq = lambda i, g, pt: (i, 0, 0)
    in_specs = [pl.BlockSpec((None, rows, KV_LORA), seq), pl.BlockSpec((None, rows, QK_ROPE), seq),
                pl.BlockSpec((None, SUBLANES, KV_LORA), seq), pl.BlockSpec((None, SUBLANES, QK_ROPE), seq)]
    in_specs += [page_spec(KV_LORA, j) for j in range(PAGES_PER_STEP)]
    in_specs += [page_spec(QK_ROPE, j) for j in range(PAGES_PER_STEP)]
    grid_spec = pltpu.PrefetchScalarGridSpec(
        num_scalar_prefetch=1, grid=(b, n_groups), in_specs=in_specs,
        out_specs=pl.BlockSpec((None, rows, KV_LORA), seq),
        scratch_shapes=[pltpu.VMEM((rows, 1), F32), pltpu.VMEM((rows, 1), F32),
                        pltpu.VMEM((rows, KV_LORA), F32)],
    )
    args = [page_table, q_lat, q_rope, c_new, r_new] + [cache_c] * PAGES_PER_STEP + [cache_r] * PAGES_PER_STEP
    return pl.pallas_call(
        functools.partial(_decode_kernel, n_new), grid_spec=grid_spec,
        out_shape=jax.ShapeDtypeStruct((b, rows, KV_LORA), F32),
        compiler_params=_cparams(("parallel", "arbitrary")), name="decode_attn",
    )(*args)


def _out_proj_kernel(x_ref, o_ref, w_ref, g_ref, b_ref, y_ref):
    mix = _dot(o_ref[...], w_ref[...])
    y_ref[...] = _layer_norm(DEEPNORM_ALPHA * x_ref[...] + mix, g_ref[...], b_ref[...])


def _out_proj_ln(x, o, w, g, b):
    t = x.shape[0]
    const = lambda i: (0, 0)
    row = lambda i: (i, 0)
    return pl.pallas_call(
        _out_proj_kernel, grid=(t // TM,),
        in_specs=[pl.BlockSpec((TM, D_MODEL), row), pl.BlockSpec((TM, N_HEADS * V_HEAD), row),
                  pl.BlockSpec(w.shape, const), pl.BlockSpec((1, D_MODEL), const),
                  pl.BlockSpec((1, D_MODEL), const)],
        out_specs=pl.BlockSpec((TM, D_MODEL), row),
        out_shape=jax.ShapeDtypeStruct((t, D_MODEL), F32),
        compiler_params=_cparams(("parallel",)), name="attn_out_ln",
    )(x, o, w, g, b)


def _causal_conv(buf, u, halo_rows, stride, w_ref):
    n = u.shape[0]
    buf[halo_rows:halo_rows + n, :] = u
    y = w_ref[CONV_W - 1:CONV_W, :] * u
    for j in range(CONV_W - 1):
        back = (CONV_W - 1 - j) * stride
        y = y + w_ref[j:j + 1, :] * buf[halo_rows - back:halo_rows - back + n, :]
    return y


def _ffn_kernel(tiles_per_seq, halo_rows, stride, x_ref, wua_ref, wug_ref, wca_ref, wcg_ref,
                ba_ref, bg_ref, wd_ref, g_ref, b_ref, *rest):
    sample = tiles_per_seq is None
    if sample:
        sa_ref, sg_ref, y_ref, la_ref, lg_ref, buf_a, buf_g, acc = rest
    else:
        y_ref, la_ref, lg_ref, buf_a, buf_g, acc, car_a, car_g = rest
    i = pl.program_id(0)
    j = pl.program_id(1)
    xb = x_ref[...].astype(BF16)
    ha = _dot(xb, wua_ref[...])
    hg = _dot(xb, wug_ref[...])
    n = ha.shape[0]
    keep = halo_rows
    if sample:
        buf_a[0:halo_rows, :] = sa_ref[...]
        buf_g[0:halo_rows, :] = sg_ref[...]
    else:
        first = (i % tiles_per_seq) == 0

        @pl.when(first)
        def _():
            buf_a[0:halo_rows, :] = jnp.zeros((halo_rows, FF_CHUNK), F32)
            buf_g[0:halo_rows, :] = jnp.zeros((halo_rows, FF_CHUNK), F32)

        @pl.when(jnp.logical_not(first))
        def _():
            buf_a[0:halo_rows, :] = car_a[j]
            buf_g[0:halo_rows, :] = car_g[j]

        car_a[j] = ha[n - keep:, :]
        car_g[j] = hg[n - keep:, :]
    la_ref[...] = ha[n - keep:, :]
    lg_ref[...] = hg[n - keep:, :]
    ya = _causal_conv(buf_a, ha, halo_rows, stride, wca_ref) + ba_ref[...]
    yg = _causal_conv(buf_g, hg, halo_rows, stride, wcg_ref) + bg_ref[...]
    act = (yg * jax.nn.sigmoid(yg) * ya).astype(BF16)
    part = _dot(act, wd_ref[...])

    @pl.when(j == 0)
    def _():
        acc[...] = part

    @pl.when(j > 0)
    def _():
        acc[...] = acc[...] + part

    @pl.when(j == pl.num_programs(1) - 1)
    def _():
        y_ref[...] = _layer_norm(DEEPNORM_ALPHA * x_ref[...] + acc[...], g_ref[...], b_ref[...])


def _conv_ffn(x, layer, w, state=None, tiles_per_seq=None, stride=1):
    t = x.shape[0]
    n_tiles = t // TM
    sample = state is not None
    halo_rows = (CONV_W - 1) * stride if sample else HALO_PROMPT
    nc = N_FF_CHUNKS
    in_specs = [
        pl.BlockSpec((TM, D_MODEL), lambda i, j: (i, 0)),
        pl.BlockSpec((None, D_MODEL, FF_CHUNK), lambda i, j: (layer, 0, j)),
        pl.BlockSpec((None, D_MODEL, FF_CHUNK), lambda i, j: (layer, 0, nc + j)),
        pl.BlockSpec((None, CONV_W, FF_CHUNK), lambda i, j: (layer, 0, j)),
        pl.BlockSpec((None, CONV_W, FF_CHUNK), lambda i, j: (layer, 0, nc + j)),
        pl.BlockSpec((None, 1, FF_CHUNK), lambda i, j: (layer, 0, j)),
        pl.BlockSpec((None, 1, FF_CHUNK), lambda i, j: (layer, 0, nc + j)),
        pl.BlockSpec((None, FF_CHUNK, D_MODEL), lambda i, j: (layer, j, 0)),
        pl.BlockSpec((None, 1, D_MODEL), lambda i, j: (2 * layer + 1, 0, 0)),
        pl.BlockSpec((None, 1, D_MODEL), lambda i, j: (2 * layer + 1, 0, 0)),
    ]
    args = [x, w['w_up'], w['w_up'], w['w_fconv'], w['w_fconv'], w['b_fconv'], w['b_fconv'],
            w['w_down'], w['ln_g'], w['ln_b']]
    scratch = [pltpu.VMEM((halo_rows + TM, FF_CHUNK), F32), pltpu.VMEM((halo_rows + TM, FF_CHUNK), F32),
               pltpu.VMEM((TM, D_MODEL), F32)]
    if sample:
        in_specs += [pl.BlockSpec((halo_rows, FF_CHUNK), lambda i, j: (0, j)),
                     pl.BlockSpec((halo_rows, FF_CHUNK), lambda i, j: (0, nc + j))]
        args += [state, state]
    else:
        scratch += [pltpu.VMEM((nc, halo_rows, FF_CHUNK), F32), pltpu.VMEM((nc, halo_rows, FF_CHUNK), F32)]
    out_shape = [jax.ShapeDtypeStruct((t, D_MODEL), F32),
                 jax.ShapeDtypeStruct((n_tiles, halo_rows, D_FF), F32),
                 jax.ShapeDtypeStruct((n_tiles, halo_rows, D_FF), F32)]
    out_specs = [pl.BlockSpec((TM, D_MODEL), lambda i, j: (i, 0)),
                 pl.BlockSpec((None, halo_rows, FF_CHUNK), lambda i, j: (i, 0, j)),
                 pl.BlockSpec((None, halo_rows, FF_CHUNK), lambda i, j: (i, 0, j))]
    return pl.pallas_call(
        functools.partial(_ffn_kernel, None if sample else tiles_per_seq, halo_rows, stride),
        grid=(n_tiles, nc), in_specs=in_specs, out_specs=out_specs, out_shape=out_shape,
        scratch_shapes=scratch,
        compiler_params=_cparams(("arbitrary", "arbitrary")), name="conv_ffn",
    )(*args)


def _sconv_kernel(tiles_per_seq, halo_rows, stride, x_ref, win_ref, wc_ref, wout_ref, g_ref, b_ref,
                  *rest):
    sample = tiles_per_seq is None
    if sample:
        s_ref, y_ref, last_ref, buf = rest
    else:
        y_ref, last_ref, buf, car = rest
    i = pl.program_id(0)
    x = x_ref[...]
    bch = _dot(x.astype(BF16), win_ref[...])
    g_b = bch[:, :D_MODEL]
    u = bch[:, D_MODEL:2 * D_MODEL] * bch[:, 2 * D_MODEL:]
    n = u.shape[0]
    if sample:
        buf[0:halo_rows, :] = s_ref[...]
    else:
        first = (i % tiles_per_seq) == 0

        @pl.when(first)
        def _():
            buf[0:halo_rows, :] = jnp.zeros((halo_rows, D_MODEL), F32)

        @pl.when(jnp.logical_not(first))
        def _():
            buf[0:halo_rows, :] = car[...]

        car[...] = u[n - halo_rows:, :]
    last_ref[...] = u[n - halo_rows:, :]
    y = _causal_conv(buf, u, halo_rows, stride, wc_ref)
    mix = _dot((g_b * y).astype(BF16), wout_ref[...])
    y_ref[...] = _layer_norm(DEEPNORM_ALPHA * x + mix, g_ref[...], b_ref[...])


def _short_conv(x, w, state=None, tiles_per_seq=None, stride=1):
    t = x.shape[0]
    n_tiles = t // TM
    sample = state is not None
    halo_rows = (CONV_W - 1) * stride if sample else HALO_PROMPT
    const = lambda i: (0, 0)
    row = lambda i: (i, 0)
    in_specs = [pl.BlockSpec((TM, D_MODEL), row), pl.BlockSpec(w['w_b_in'].shape, const),
                pl.BlockSpec((CONV_W, D_MODEL), const), pl.BlockSpec(w['w_b_out'].shape, const),
                pl.BlockSpec((None, 1, D_MODEL), lambda i: (2, 0, 0)),
                pl.BlockSpec((None, 1, D_MODEL), lambda i: (2, 0, 0))]
    args = [x, w['w_b_in'], w['w_b_conv'], w['w_b_out'], w['ln_g'], w['ln_b']]
    scratch = [pltpu.VMEM((halo_rows + TM, D_MODEL), F32)]
    if sample:
        in_specs += [pl.BlockSpec((halo_rows, D_MODEL), const)]
        args += [state]
    else:
        scratch += [pltpu.VMEM((halo_rows, D_MODEL), F32)]
    return pl.pallas_call(
        functools.partial(_sconv_kernel, None if sample else tiles_per_seq, halo_rows, stride),
        grid=(n_tiles,), in_specs=in_specs,
        out_specs=[pl.BlockSpec((TM, D_MODEL), row),
                   pl.BlockSpec((None, halo_rows, D_MODEL), lambda i: (i, 0, 0))],
        out_shape=[jax.ShapeDtypeStruct((t, D_MODEL), F32),
                   jax.ShapeDtypeStruct((n_tiles, halo_rows, D_MODEL), F32)],
        scratch_shapes=scratch,
        compiler_params=_cparams(("arbitrary",)), name="short_conv",
    )(*args)


def _rot_cols(w):
    return jnp.concatenate([-w[..., HALF_ROPE:], w[..., :HALF_ROPE]], axis=-1)


def _prepare_weights(w_a_down, g_q_norm, g_kv_norm, w_a_uq, w_a_uk, w_a_uv, w_a_out,
                     w_b_in, w_b_conv, w_b_out, w_f_up, w_f_conv, b_f_conv, w_f_down, ln_g, ln_b):
    o = Q_LORA + KV_LORA
    w_kr = w_a_down[:, o:]
    zpad = jnp.zeros((D_MODEL, LANES - QK_ROPE), F32)
    wd = jnp.concatenate([w_a_down[:, :o], w_kr, zpad, _rot_cols(w_kr), zpad], axis=1)

    wq3 = w_a_uq.reshape(Q_LORA, N_HEADS, QK_HEAD)
    zq = jnp.zeros((Q_LORA, N_HEADS, HEAD_PAD - QK_HEAD), F32)
    q_main = jnp.concatenate([wq3, zq], axis=-1)
    q_rot = jnp.concatenate([jnp.zeros((Q_LORA, N_HEADS, QK_NOPE), F32), _rot_cols(wq3[..., QK_NOPE:]), zq],
                            axis=-1)
    wq = jnp.concatenate([q_main.reshape(Q_LORA, -1), q_rot.reshape(Q_LORA, -1)], axis=1)

    wk = jnp.concatenate([w_a_uk, jnp.zeros((KV_LORA, N_HEADS, HEAD_PAD - QK_NOPE), F32)], axis=-1)
    wk = wk.reshape(KV_LORA, N_HEADS * HEAD_PAD)
    src = jnp.arange(LANES)[:, None]
    dst = jnp.arange(N_HEADS * HEAD_PAD)[None, :]
    e = ((src < QK_ROPE) & ((dst % HEAD_PAD) == QK_NOPE + src)).astype(BF16)

    w_ukt = jnp.transpose(w_a_uk, (1, 2, 0))
    w_ukt = jnp.concatenate([w_ukt, jnp.zeros((N_HEADS, HEAD_PAD - QK_NOPE, KV_LORA), F32)], axis=1)
    w_uvh = jnp.transpose(w_a_uv, (1, 0, 2))
    zv = jnp.zeros_like(w_uvh)
    even = (jnp.arange(N_HEADS) % 2 == 0)[:, None, None]
    w_uv_pair = jnp.where(even, jnp.concatenate([w_uvh, zv], -1), jnp.concatenate([zv, w_uvh], -1))

    return dict(
        wd=wd.astype(BF16), gq=g_q_norm.reshape(1, -1), gkv=g_kv_norm.reshape(1, -1), wq=wq.astype(BF16),
        wk=wk.astype(BF16), e=e, wv=w_a_uv.reshape(KV_LORA, -1).astype(BF16),
        w_ukt=w_ukt.astype(BF16), w_uv_pair=w_uv_pair.astype(BF16), w_a_out=w_a_out.astype(BF16),
        w_b_in=w_b_in.astype(BF16), w_b_conv=w_b_conv, w_b_out=w_b_out.astype(BF16),
        w_up=w_f_up.astype(BF16), w_fconv=w_f_conv, b_fconv=b_f_conv.reshape(DEPTH, 1, 2 * D_FF),
        w_down=w_f_down.astype(BF16), ln_g=ln_g.reshape(2 * DEPTH, 1, D_MODEL),
        ln_b=ln_b.reshape(2 * DEPTH, 1, D_MODEL),
    )


def _rope_tables(pos):
    inv = 1.0 / (ROPE_THETA ** (jnp.arange(0, QK_ROPE, 2, dtype=F32) / QK_ROPE))
    ang = pos.astype(F32)[:, None] * inv[None, :]
    cos, sin = jnp.cos(ang), jnp.sin(ang)
    p = pos.shape[0]
    one = jnp.ones((p, QK_NOPE), F32)
    z64 = jnp.zeros((p, QK_NOPE), F32)
    z32 = jnp.zeros((p, HEAD_PAD - QK_HEAD), F32)
    z96 = jnp.zeros((p, LANES - QK_ROPE), F32)
    cosg = jnp.concatenate([one, cos, cos, z32], axis=1)
    sing = jnp.concatenate([z64, sin, sin, z32], axis=1)
    ckr = jnp.concatenate([cos, cos, z96], axis=1)
    skr = jnp.concatenate([sin, sin, z96], axis=1)
    return cosg, sing, ckr, skr


def _prompt_trunk(x_prompt, w):
    b, s, _ = x_prompt.shape
    x = x_prompt.reshape(b * s, D_MODEL)
    tps = s // TM
    tabs = _rope_tables(jnp.arange(s))
    q, c_kv, k_r, k, v = _mla_proj(x, tabs, w, True, tps)
    o = _flash_attention(q, k, v, b, s)
    x = _out_proj_ln(x, o, w['w_a_out'], w['ln_g'][0], w['ln_b'][0])
    x, la0, lg0 = _conv_ffn(x, 0, w, tiles_per_seq=tps)
    x, lc = _short_conv(x, w, tiles_per_seq=tps)
    x, la1, lg1 = _conv_ffn(x, 1, w, tiles_per_seq=tps)

    def seq_last(a):
        return a[tps - 1::tps, HALO_PROMPT - (CONV_W - 1):, :]

    ffn_state = jnp.stack([jnp.concatenate([seq_last(la0), seq_last(lg0)], axis=-1),
                           jnp.concatenate([seq_last(la1), seq_last(lg1)], axis=-1)])
    return (x.reshape(b, s, D_MODEL),
            c_kv.reshape(b, s // PAGE_SIZE, PAGE_SIZE, KV_LORA),
            k_r.reshape(b, s // PAGE_SIZE, PAGE_SIZE, QK_ROPE),
            seq_last(lc), ffn_state)


def _sample_trunk(x_sample, cache_c, cache_r, state_conv_b, state_ffn_conv, page_table, w):
    b, t, _ = x_sample.shape
    n = b * t
    past_len = page_table.shape[1] * cache_c.shape[1]
    tm_rows = lambda a: jnp.swapaxes(a, 0, 1).reshape(a.shape[1] * a.shape[0], a.shape[-1])
    bm_rows = lambda a, k: jnp.swapaxes(a.reshape(k, b, a.shape[-1]), 0, 1)
    x = tm_rows(x_sample)
    pos = past_len + jnp.repeat(jnp.arange(t), b)
    tabs = _rope_tables(pos)
    q, c_kv, k_r = _mla_proj(x, tabs, w, False, n // TM)

    q_lat = _absorb_q(q, w['w_ukt'])
    q_lat = jnp.transpose(q_lat.reshape(N_HEADS, t, b, KV_LORA), (2, 1, 0, 3)).reshape(b, t * N_HEADS, KV_LORA)
    q_rope = q.reshape(t, b, N_HEADS, HEAD_PAD)[..., QK_NOPE:QK_HEAD]
    q_rope = jnp.transpose(q_rope, (1, 0, 2, 3)).reshape(b, t * N_HEADS, QK_ROPE)
    pad_new = lambda a: jnp.pad(bm_rows(a, t), ((0, 0), (0, SUBLANES - t), (0, 0)))
    o_lat = _decode_attention(q_lat, q_rope, pad_new(c_kv), pad_new(k_r), cache_c, cache_r, page_table, t)
    o_lat = jnp.transpose(o_lat.reshape(b, t, N_HEADS, KV_LORA), (2, 1, 0, 3)).reshape(N_HEADS, n, KV_LORA)
    o = _unabsorb_o(o_lat.astype(BF16), w['w_uv_pair'])

    x = _out_proj_ln(x, o, w['w_a_out'], w['ln_g'][0], w['ln_b'][0])
    x, la0, lg0 = _conv_ffn(x, 0, w, state=tm_rows(state_ffn_conv[0]), stride=b)
    x, lc = _short_conv(x, w, state=tm_rows(state_conv_b), stride=b)
    x, la1, lg1 = _conv_ffn(x, 1, w, state=tm_rows(state_ffn_conv[1]), stride=b)

    k = CONV_W - 1
    ffn_state = jnp.stack([bm_rows(jnp.concatenate([la0[0], lg0[0]], axis=-1), k),
                           bm_rows(jnp.concatenate([la1[0], lg1[0]], axis=-1), k)])
    return (bm_rows(x, t), bm_rows(c_kv, t), bm_rows(k_r, t), bm_rows(lc[0], k), ffn_state)


def kernel(x_prompt, x_sample, cache_kv_latent, cache_k_rope, state_conv_b, state_ffn_conv, page_table,
           w_a_down, g_q_norm, g_kv_norm, w_a_uq, w_a_uk, w_a_uv, w_a_out,
           w_b_in, w_b_conv, w_b_out, w_f_up, w_f_conv, b_f_conv, w_f_down, ln_g, ln_b):
    w = _prepare_weights(w_a_down, g_q_norm, g_kv_norm, w_a_uq, w_a_uk, w_a_uv, w_a_out,
                         w_b_in, w_b_conv, w_b_out, w_f_up, w_f_conv, b_f_conv, w_f_down, ln_g, ln_b)
    y_p, lat_p, rope_p, conv_p, ffn_p = _prompt_trunk(x_prompt, w)
    y_s, lat_s, rope_s, conv_s, ffn_s = _sample_trunk(
        x_sample, cache_kv_latent, cache_k_rope, state_conv_b, state_ffn_conv, page_table, w)
    return (y_p, y_s, lat_p, rope_p, lat_s, rope_s, conv_p, conv_s, ffn_p, ffn_s)
```

```python
import functools
import math

import jax
import jax.numpy as jnp
from jax import lax
from jax.experimental import pallas as pl
from jax.experimental.pallas import tpu as pltpu

D_MODEL = 1024
N_HEADS = 16
Q_LORA = 384
KV_LORA = 256
QK_NOPE = 64
QK_ROPE = 32
HALF_ROPE = QK_ROPE // 2
V_HEAD = 64
QK_HEAD = QK_NOPE + QK_ROPE
ROPE_THETA = 10000.0
ATTN_SCALE = QK_HEAD ** -0.5
Q_SCALE = ATTN_SCALE * math.log2(math.e)
CONV_W = 3
D_FF = 2816
RMS_EPS = 1e-6
LN_EPS = 1e-5
DEPTH = 2
DEEPNORM_ALPHA = (2 * DEPTH) ** 0.25
PAGE_SIZE = 128

LANES = 128
SUBLANES = 8
HEAD_PAD = LANES
VMEM_LIMIT = 56 * 1024 * 1024

TM = 512
TQ = 512
TK = 512
RQ = 128
QK_AHEAD = 4
FF_CHUNK = 1408
N_FF_CHUNKS = D_FF // FF_CHUNK
FF_ROWS = 128
UP_AHEAD = 1
PAGES_PER_STEP = 16
HALO_PROMPT = SUBLANES

BF16 = jnp.bfloat16
F32 = jnp.float32


def _dot(a, b):
    return jnp.dot(a, b, preferred_element_type=F32)


def _dot_nt(a, b):
    return lax.dot_general(a, b, (((1,), (1,)), ((), ())), preferred_element_type=F32)


def _layer_norm(v, g, b):
    mu = jnp.mean(v, axis=-1, keepdims=True)
    vc = v - mu
    var = jnp.mean(vc * vc, axis=-1, keepdims=True)
    return vc * lax.rsqrt(var + LN_EPS) * g + b


def _rms_norm(v, g):
    return v * lax.rsqrt(jnp.mean(v * v, axis=-1, keepdims=True) + RMS_EPS) * g


def _cparams(sem):
    return pltpu.CompilerParams(dimension_semantics=sem, vmem_limit_bytes=VMEM_LIMIT)


def _mla_proj_kernel(emit_kv, x_ref, wd_ref, gq_ref, gkv_ref, wq_ref, cosg_ref, sing_ref,
                     ckr_ref, skr_ref, *rest):
    if emit_kv:
        wk_ref, e_ref, wv_ref, q_ref, ckv_ref, kr_ref, k_ref, v_ref = rest
    else:
        q_ref, ckv_ref, kr_ref = rest
    xb = x_ref[...].astype(BF16)
    d = _dot(xb, wd_ref[...])
    c_q = _rms_norm(d[:, :Q_LORA], gq_ref[...])
    c_kv = _rms_norm(d[:, Q_LORA:Q_LORA + KV_LORA], gkv_ref[...])
    ckv_ref[...] = c_kv
    o = Q_LORA + KV_LORA
    krg = d[:, o:o + LANES] * ckr_ref[...] + d[:, o + LANES:o + 2 * LANES] * skr_ref[...]
    kr_ref[...] = krg[:, :QK_ROPE]
    q2 = _dot(c_q.astype(BF16), wq_ref[...])
    cosg = cosg_ref[...] * Q_SCALE
    sing = sing_ref[...] * Q_SCALE
    qw = N_HEADS * HEAD_PAD
    for h in range(N_HEADS):
        lo = h * HEAD_PAD
        q_ref[:, lo:lo + HEAD_PAD] = (q2[:, lo:lo + HEAD_PAD] * cosg
                                      + q2[:, qw + lo:qw + lo + HEAD_PAD] * sing).astype(BF16)
    if emit_kv:
        ckvb = c_kv.astype(BF16)
        kt = _dot_nt(wk_ref[...], ckvb) + _dot_nt(e_ref[...], krg.astype(BF16))
        k_ref[...] = kt.astype(BF16)
        v_ref[...] = _dot(ckvb, wv_ref[...]).astype(BF16)


def _mla_proj(x, tabs, w, emit_kv, pos_tiles):
    t = x.shape[0]
    n = t // TM
    qw = N_HEADS * HEAD_PAD
    const = lambda i: (0, 0)
    row = lambda i: (i, 0)
    tab = lambda i: (i % pos_tiles, 0)
    in_specs = [
        pl.BlockSpec((TM, D_MODEL), row),
        pl.BlockSpec(w['wd'].shape, const),
        pl.BlockSpec((1, Q_LORA), const),
        pl.BlockSpec((1, KV_LORA), const),
        pl.BlockSpec(w['wq'].shape, const),
    ] + [pl.BlockSpec((TM, LANES), tab)] * 4
    args = [x, w['wd'], w['gq'], w['gkv'], w['wq']] + list(tabs)
    out_shape = [jax.ShapeDtypeStruct((t, qw), BF16),
                 jax.ShapeDtypeStruct((t, KV_LORA), F32),
                 jax.ShapeDtypeStruct((t, QK_ROPE), F32)]
    out_specs = [pl.BlockSpec((TM, qw), row), pl.BlockSpec((TM, KV_LORA), row),
                 pl.BlockSpec((TM, QK_ROPE), row)]
    if emit_kv:
        in_specs += [pl.BlockSpec(w['wk'].shape, const), pl.BlockSpec(w['e'].shape, const),
                     pl.BlockSpec(w['wv'].shape, const)]
        args += [w['wk'], w['e'], w['wv']]
        out_shape += [jax.ShapeDtypeStruct((qw, t), BF16),
                      jax.ShapeDtypeStruct((t, N_HEADS * V_HEAD), BF16)]
        out_specs += [pl.BlockSpec((qw, TM), lambda i: (0, i)),
                      pl.BlockSpec((TM, N_HEADS * V_HEAD), row)]
    return pl.pallas_call(
        functools.partial(_mla_proj_kernel, emit_kv),
        grid=(n,), in_specs=in_specs, out_specs=out_specs, out_shape=out_shape,
        compiler_params=_cparams(("parallel",)), name="mla_proj_kv" if emit_kv else "mla_proj",
    )(*args)


def _flash_kernel(qi_tab, ki_tab, q_ref, k_ref, v_ref, o_ref, m_sc, l_sc, acc_sc):
    step = pl.program_id(2)
    qi = qi_tab[step]
    ki = ki_tab[step]

    @pl.when(ki == 0)
    def _():
        m_sc[...] = jnp.full(m_sc.shape, -jnp.inf, F32)
        l_sc[...] = jnp.zeros(l_sc.shape, F32)
        acc_sc[...] = jnp.zeros(acc_sc.shape, F32)

    def tile(diagonal):
        chains = [(hh, r) for hh in range(2) for r in range(TQ // RQ)]

        def n_keys(r):
            return (r + 1) * RQ if diagonal else TK

        def scores(chain):
            hh, r = chain
            hs = slice(hh * HEAD_PAD, (hh + 1) * HEAD_PAD)
            return _dot(q_ref[r * RQ:(r + 1) * RQ, hs], k_ref[hs, 0:n_keys(r)])

        def finish(chain, s):
            hh, r = chain
            rows = slice(r * RQ, (r + 1) * RQ)
            nk = n_keys(r)
            if diagonal:
                row = lax.broadcasted_iota(jnp.int32, (RQ, nk), 0) + r * RQ
                col = lax.broadcasted_iota(jnp.int32, (RQ, nk), 1)
                s = jnp.where(col <= row, s, -jnp.inf)
            chunks = [s[:, c * LANES:(c + 1) * LANES] for c in range(nk // LANES)]
            mx = chunks[0]
            for c in chunks[1:]:
                mx = jnp.maximum(mx, c)
            m_prev = m_sc[hh, rows, :]
            m_new = jnp.maximum(m_prev, jnp.max(mx, axis=-1, keepdims=True))
            alpha = jnp.exp2(m_prev - m_new)
            ps = [jnp.exp2(c - m_new) for c in chunks]
            psum = ps[0]
            for pc in ps[1:]:
                psum = psum + pc
            p = jnp.concatenate(ps, axis=1).astype(BF16) if len(ps) > 1 else ps[0].astype(BF16)
            l_sc[hh, rows, :] = alpha * l_sc[hh, rows, :] + psum
            acc_sc[hh, rows, :] = alpha * acc_sc[hh, rows, :] + _dot(p, v_ref[0:nk, :])
            m_sc[hh, rows, :] = m_new

        pending = [scores(c) for c in chains[:QK_AHEAD]]
        for i, chain in enumerate(chains):
            s = pending.pop(0)
            if i + QK_AHEAD < len(chains):
                pending.append(scores(chains[i + QK_AHEAD]))
            finish(chain, s)

    @pl.when(ki < qi)
    def _():
        tile(False)

    @pl.when(ki == qi)
    def _():
        tile(True)
        lane = lax.broadcasted_iota(jnp.int32, (TQ, 2 * V_HEAD), 1)
        o0 = acc_sc[0] / jnp.sum(l_sc[0], axis=-1, keepdims=True)
        o1 = acc_sc[1] / jnp.sum(l_sc[1], axis=-1, keepdims=True)
        o_ref[...] = jnp.where(lane < V_HEAD, o0, o1).astype(BF16)


def _flash_attention(q, k, v, batch, seq):
    nq = seq // TQ
    pairs = [(a, b) for a in range(nq) for b in range(a + 1)]
    qi_tab = jnp.asarray([p[0] for p in pairs], jnp.int32)
    ki_tab = jnp.asarray([p[1] for p in pairs], jnp.int32)
    grid_spec = pltpu.PrefetchScalarGridSpec(
        num_scalar_prefetch=2,
        grid=(batch, N_HEADS // 2, len(pairs)),
        in_specs=[
            pl.BlockSpec((TQ, 2 * HEAD_PAD), lambda b, h, s, qt, kt: (b * nq + qt[s], h)),
            pl.BlockSpec((2 * HEAD_PAD, TK), lambda b, h, s, qt, kt: (h, b * nq + kt[s])),
            pl.BlockSpec((TK, 2 * V_HEAD), lambda b, h, s, qt, kt: (b * nq + kt[s], h)),
        ],
        out_specs=pl.BlockSpec((TQ, 2 * V_HEAD), lambda b, h, s, qt, kt: (b * nq + qt[s], h)),
        scratch_shapes=[pltpu.VMEM((2, TQ, LANES), F32), pltpu.VMEM((2, TQ, LANES), F32),
                        pltpu.VMEM((2, TQ, 2 * V_HEAD), F32)],
    )
    return pl.pallas_call(
        _flash_kernel, grid_spec=grid_spec,
        out_shape=jax.ShapeDtypeStruct((batch * seq, N_HEADS * V_HEAD), BF16),
        compiler_params=_cparams(("parallel", "parallel", "arbitrary")), name="flash_attn",
    )(qi_tab, ki_tab, q, k, v)


def _absorb_kernel(q_ref, w_ref, o_ref):
    o_ref[...] = _dot(q_ref[...], w_ref[...]).astype(BF16)


def _absorb_q(q, w_ukt):
    t = q.shape[0]
    return pl.pallas_call(
        _absorb_kernel, grid=(N_HEADS,),
        in_specs=[pl.BlockSpec((t, HEAD_PAD), lambda h: (0, h)),
                  pl.BlockSpec((None, HEAD_PAD, KV_LORA), lambda h: (h, 0, 0))],
        out_specs=pl.BlockSpec((None, t, KV_LORA), lambda h: (h, 0, 0)),
        out_shape=jax.ShapeDtypeStruct((N_HEADS, t, KV_LORA), BF16),
        compiler_params=_cparams(("parallel",)), name="absorb_q",
    )(q, w_ukt)


def _unabsorb_kernel(o_ref, w_ref, out_ref):
    out_ref[...] = (_dot(o_ref[0], w_ref[0]) + _dot(o_ref[1], w_ref[1])).astype(BF16)


def _unabsorb_o(o_lat, w_uv_pair):
    t = o_lat.shape[1]
    return pl.pallas_call(
        _unabsorb_kernel, grid=(N_HEADS // 2,),
        in_specs=[pl.BlockSpec((2, t, KV_LORA), lambda j: (j, 0, 0)),
                  pl.BlockSpec((2, KV_LORA, 2 * V_HEAD), lambda j: (j, 0, 0))],
        out_specs=pl.BlockSpec((t, 2 * V_HEAD), lambda j: (0, j)),
        out_shape=jax.ShapeDtypeStruct((t, N_HEADS * V_HEAD), BF16),
        compiler_params=_cparams(("parallel",)), name="unabsorb_o",
    )(o_lat, w_uv_pair)


def _decode_kernel(n_new, pt_ref, ql_ref, qr_ref, cn_ref, rn_ref, *rest):
    c_refs = rest[:PAGES_PER_STEP]
    r_refs = rest[PAGES_PER_STEP:2 * PAGES_PER_STEP]
    o_ref, m_sc, l_sc, acc_sc = rest[2 * PAGES_PER_STEP:]
    g = pl.program_id(1)
    rows = ql_ref.shape[0]

    @pl.when(g == 0)
    def _():
        m_sc[...] = jnp.full(m_sc.shape, -jnp.inf, F32)
        l_sc[...] = jnp.zeros(l_sc.shape, F32)
        acc_sc[...] = jnp.zeros(acc_sc.shape, F32)

    ql = ql_ref[...]
    qr = qr_ref[...]
    cs = [c_refs[j][...].astype(BF16) for j in range(PAGES_PER_STEP)]
    ss = [_dot_nt(ql, cs[j]) + _dot(qr, r_refs[j][...].astype(BF16))
          for j in range(PAGES_PER_STEP)]
    mx = ss[0]
    for j in range(1, PAGES_PER_STEP):
        mx = jnp.maximum(mx, ss[j])
    m_prev = m_sc[...]
    m_new = jnp.maximum(m_prev, jnp.max(mx, axis=-1, keepdims=True))
    alpha = jnp.exp2(m_prev - m_new)
    psum = jnp.zeros((rows, PAGE_SIZE), F32)
    pv = jnp.zeros((rows, KV_LORA), F32)
    for j in range(PAGES_PER_STEP):
        p = jnp.exp2(ss[j] - m_new)
        psum = psum + p
        pv = pv + _dot(p.astype(BF16), cs[j])
    l_sc[...] = alpha * l_sc[...] + jnp.sum(psum, axis=-1, keepdims=True)
    acc_sc[...] = alpha * acc_sc[...] + pv
    m_sc[...] = m_new

    @pl.when(g == pl.num_programs(1) - 1)
    def _():
        qlf = ql.astype(F32)
        qrf = qr.astype(F32)
        row = lax.broadcasted_iota(jnp.int32, (rows, 1), 0)
        s_new = []
        for j in range(n_new):
            sj = (jnp.sum(qlf * cn_ref[j:j + 1, :], axis=-1, keepdims=True)
                  + jnp.sum(qrf * rn_ref[j:j + 1, :], axis=-1, keepdims=True))
            s_new.append(jnp.where(row >= j * N_HEADS, sj, -jnp.inf))
        m_old = m_sc[...]
        m_fin = m_old
        for sj in s_new:
            m_fin = jnp.maximum(m_fin, sj)
        a = jnp.exp2(m_old - m_fin)
        l_fin = a * l_sc[...]
        acc = a * acc_sc[...]
        for j in range(n_new):
            pj = jnp.exp2(s_new[j] - m_fin)
            l_fin = l_fin + pj
            acc = acc + pj * cn_ref[j:j + 1, :]
        o_ref[...] = acc / l_fin


def _decode_attention(q_lat, q_rope, c_new, r_new, cache_c, cache_r, page_table, n_new):
    b, rows, _ = q_lat.shape
    n_pages = page_table.shape[1]
    n_groups = n_pages // PAGES_PER_STEP

    def page_spec(shape, j):
        return pl.BlockSpec((None,) + shape,
                            lambda i, g, pt: (pt[i, g * PAGES_PER_STEP + j], 0, 0))

    seq = lambda i, g, pt: (i, 0, 0)
    in_specs = [pl.BlockSpec((None, rows, KV_LORA), seq), pl.BlockSpec((None, rows, QK_ROPE), seq),
                pl.BlockSpec((None, SUBLANES, KV_LORA), seq), pl.BlockSpec((None, SUBLANES, QK_ROPE), seq)]
    in_specs += [page_spec((PAGE_SIZE, KV_LORA), j) for j in range(PAGES_PER_STEP)]
    in_specs += [page_spec((QK_ROPE, PAGE_SIZE), j) for j in range(PAGES_PER_STEP)]
    grid_spec = pltpu.PrefetchScalarGridSpec(
        num_scalar_prefetch=1, grid=(b, n_groups), in_specs=in_specs,
        out_specs=pl.BlockSpec((None, rows, KV_LORA), seq),
        scratch_shapes=[pltpu.VMEM((rows, 1), F32), pltpu.VMEM((rows, 1), F32),
                        pltpu.VMEM((rows, KV_LORA), F32)],
    )
    args = [page_table, q_lat, q_rope, c_new, r_new] + [cache_c] * PAGES_PER_STEP + [cache_r] * PAGES_PER_STEP
    return pl.pallas_call(
        functools.partial(_decode_kernel, n_new), grid_spec=grid_spec,
        out_shape=jax.ShapeDtypeStruct((b, rows, KV_LORA), F32),
        compiler_params=_cparams(("parallel", "arbitrary")), name="decode_attn",
    )(*args)


def _out_proj_kernel(x_ref, o_ref, w_ref, g_ref, b_ref, y_ref):
    mix = _dot(o_ref[...], w_ref[...])
    y_ref[...] = _layer_norm(DEEPNORM_ALPHA * x_ref[...] + mix, g_ref[...], b_ref[...])


def _out_proj_ln(x, o, w, g, b):
    t = x.shape[0]
    const = lambda i: (0, 0)
    row = lambda i: (i, 0)
    return pl.pallas_call(
        _out_proj_kernel, grid=(t // TM,),
        in_specs=[pl.BlockSpec((TM, D_MODEL), row), pl.BlockSpec((TM, N_HEADS * V_HEAD), row),
                  pl.BlockSpec(w.shape, const), pl.BlockSpec((1, D_MODEL), const),
                  pl.BlockSpec((1, D_MODEL), const)],
        out_specs=pl.BlockSpec((TM, D_MODEL), row),
        out_shape=jax.ShapeDtypeStruct((t, D_MODEL), F32),
        compiler_params=_cparams(("parallel",)), name="attn_out_ln",
    )(x, o, w, g, b)


def _causal_conv(buf, u, row0, halo_rows, stride, w_ref):
    n = u.shape[0]
    base = halo_rows + row0
    buf[base:base + n, :] = u
    y = w_ref[CONV_W - 1:CONV_W, :] * u
    for j in range(CONV_W - 1):
        back = (CONV_W - 1 - j) * stride
        y = y + w_ref[j:j + 1, :] * buf[base - back:base - back + n, :]
    return y


def _ffn_kernel(tiles_per_seq, halo_rows, stride, x_ref, wua_ref, wug_ref, wca_ref, wcg_ref,
                ba_ref, bg_ref, wd_ref, g_ref, b_ref, *rest):
    sample = tiles_per_seq is None
    if sample:
        sa_ref, sg_ref, y_ref, la_ref, lg_ref, buf_a, buf_g, acc = rest
    else:
        y_ref, la_ref, lg_ref, buf_a, buf_g, acc, car_a, car_g = rest
    i = pl.program_id(0)
    j = pl.program_id(1)
    n = x_ref.shape[0]
    if sample:
        buf_a[0:halo_rows, :] = sa_ref[...]
        buf_g[0:halo_rows, :] = sg_ref[...]
    else:
        first = (i % tiles_per_seq) == 0

        @pl.when(first)
        def _():
            buf_a[0:halo_rows, :] = jnp.zeros((halo_rows, FF_CHUNK), F32)
            buf_g[0:halo_rows, :] = jnp.zeros((halo_rows, FF_CHUNK), F32)

        @pl.when(jnp.logical_not(first))
        def _():
            buf_a[0:halo_rows, :] = car_a[j]
            buf_g[0:halo_rows, :] = car_g[j]

    @pl.when(j == 0)
    def _():
        acc[...] = jnp.zeros(acc.shape, F32)

    def up(r):
        xb = x_ref[r * FF_ROWS:(r + 1) * FF_ROWS, :].astype(BF16)
        return _dot(xb, wua_ref[...]), _dot(xb, wug_ref[...])

    def down(r, ha, hg):
        rows = slice(r * FF_ROWS, (r + 1) * FF_ROWS)
        ya = _causal_conv(buf_a, ha, r * FF_ROWS, halo_rows, stride, wca_ref) + ba_ref[...]
        yg = _causal_conv(buf_g, hg, r * FF_ROWS, halo_rows, stride, wcg_ref) + bg_ref[...]
        act = (yg * jax.nn.sigmoid(yg) * ya).astype(BF16)
        acc[rows, :] = acc[rows, :] + _dot(act, wd_ref[...])

    n_blocks = n // FF_ROWS
    pending = [up(r) for r in range(min(UP_AHEAD, n_blocks))]
    for r in range(n_blocks):
        ha, hg = pending.pop(0)
        if r + UP_AHEAD < n_blocks:
            pending.append(up(r + UP_AHEAD))
        down(r, ha, hg)

    last_a = buf_a[n:n + halo_rows, :]
    last_g = buf_g[n:n + halo_rows, :]
    la_ref[...] = last_a
    lg_ref[...] = last_g
    if not sample:
        car_a[j] = last_a
        car_g[j] = last_g

    @pl.when(j == pl.num_programs(1) - 1)
    def _():
        y_ref[...] = _layer_norm(DEEPNORM_ALPHA * x_ref[...] + acc[...], g_ref[...], b_ref[...])


def _conv_ffn(x, layer, w, state=None, tiles_per_seq=None, stride=1):
    t = x.shape[0]
    n_tiles = t // TM
    sample = state is not None
    halo_rows = (CONV_W - 1) * stride if sample else HALO_PROMPT
    nc = N_FF_CHUNKS
    in_specs = [
        pl.BlockSpec((TM, D_MODEL), lambda i, j: (i, 0)),
        pl.BlockSpec((None, D_MODEL, FF_CHUNK), lambda i, j: (layer, 0, j)),
        pl.BlockSpec((None, D_MODEL, FF_CHUNK), lambda i, j: (layer, 0, nc + j)),
        pl.BlockSpec((None, CONV_W, FF_CHUNK), lambda i, j: (layer, 0, j)),
        pl.BlockSpec((None, CONV_W, FF_CHUNK), lambda i, j: (layer, 0, nc + j)),
        pl.BlockSpec((None, 1, FF_CHUNK), lambda i, j: (layer, 0, j)),
        pl.BlockSpec((None, 1, FF_CHUNK), lambda i, j: (layer, 0, nc + j)),
        pl.BlockSpec((None, FF_CHUNK, D_MODEL), lambda i, j: (layer, j, 0)),
        pl.BlockSpec((None, 1, D_MODEL), lambda i, j: (2 * layer + 1, 0, 0)),
        pl.BlockSpec((None, 1, D_MODEL), lambda i, j: (2 * layer + 1, 0, 0)),
    ]
    args = [x, w['w_up'], w['w_up'], w['w_fconv'], w['w_fconv'], w['b_fconv'], w['b_fconv'],
            w['w_down'], w['ln_g'], w['ln_b']]
    scratch = [pltpu.VMEM((halo_rows + TM, FF_CHUNK), F32), pltpu.VMEM((halo_rows + TM, FF_CHUNK), F32),
               pltpu.VMEM((TM, D_MODEL), F32)]
    if sample:
        in_specs += [pl.BlockSpec((halo_rows, FF_CHUNK), lambda i, j: (0, j)),
                     pl.BlockSpec((halo_rows, FF_CHUNK), lambda i, j: (0, nc + j))]
        args += [state, state]
    else:
        scratch += [pltpu.VMEM((nc, halo_rows, FF_CHUNK), F32), pltpu.VMEM((nc, halo_rows, FF_CHUNK), F32)]
    out_shape = [jax.ShapeDtypeStruct((t, D_MODEL), F32),
                 jax.ShapeDtypeStruct((n_tiles, halo_rows, D_FF), F32),
                 jax.ShapeDtypeStruct((n_tiles, halo_rows, D_FF), F32)]
    out_specs = [pl.BlockSpec((TM, D_MODEL), lambda i, j: (i, 0)),
                 pl.BlockSpec((None, halo_rows, FF_CHUNK), lambda i, j: (i, 0, j)),
                 pl.BlockSpec((None, halo_rows, FF_CHUNK), lambda i, j: (i, 0, j))]
    return pl.pallas_call(
        functools.partial(_ffn_kernel, None if sample else tiles_per_seq, halo_rows, stride),
        grid=(n_tiles, nc), in_specs=in_specs, out_specs=out_specs, out_shape=out_shape,
        scratch_shapes=scratch,
        compiler_params=_cparams(("arbitrary", "arbitrary")), name="conv_ffn",
    )(*args)


def _sconv_kernel(tiles_per_seq, halo_rows, stride, x_ref, win_ref, wc_ref, wout_ref, g_ref, b_ref,
                  *rest):
    sample = tiles_per_seq is None
    if sample:
        s_ref, y_ref, last_ref, buf = rest
    else:
        y_ref, last_ref, buf, car = rest
    i = pl.program_id(0)
    x = x_ref[...]
    bch = _dot(x.astype(BF16), win_ref[...])
    g_b = bch[:, :D_MODEL]
    u = bch[:, D_MODEL:2 * D_MODEL] * bch[:, 2 * D_MODEL:]
    n = u.shape[0]
    if sample:
        buf[0:halo_rows, :] = s_ref[...]
    else:
        first = (i % tiles_per_seq) == 0

        @pl.when(first)
        def _():
            buf[0:halo_rows, :] = jnp.zeros((halo_rows, D_MODEL), F32)

        @pl.when(jnp.logical_not(first))
        def _():
            buf[0:halo_rows, :] = car[...]

        car[...] = u[n - halo_rows:, :]
    last_ref[...] = u[n - halo_rows:, :]
    y = _causal_conv(buf, u, 0, halo_rows, stride, wc_ref)
    mix = _dot((g_b * y).astype(BF16), wout_ref[...])
    y_ref[...] = _layer_norm(DEEPNORM_ALPHA * x + mix, g_ref[...], b_ref[...])


def _short_conv(x, w, state=None, tiles_per_seq=None, stride=1):
    t = x.shape[0]
    n_tiles = t // TM
    sample = state is not None
    halo_rows = (CONV_W - 1) * stride if sample else HALO_PROMPT
    const = lambda i: (0, 0)
    row = lambda i: (i, 0)
    in_specs = [pl.BlockSpec((TM, D_MODEL), row), pl.BlockSpec(w['w_b_in'].shape, const),
                pl.BlockSpec((CONV_W, D_MODEL), const), pl.BlockSpec(w['w_b_out'].shape, const),
                pl.BlockSpec((None, 1, D_MODEL), lambda i: (2, 0, 0)),
                pl.BlockSpec((None, 1, D_MODEL), lambda i: (2, 0, 0))]
    args = [x, w['w_b_in'], w['w_b_conv'], w['w_b_out'], w['ln_g'], w['ln_b']]
    scratch = [pltpu.VMEM((halo_rows + TM, D_MODEL), F32)]
    if sample:
        in_specs += [pl.BlockSpec((halo_rows, D_MODEL), const)]
        args += [state]
    else:
        scratch += [pltpu.VMEM((halo_rows, D_MODEL), F32)]
    return pl.pallas_call(
        functools.partial(_sconv_kernel, None if sample else tiles_per_seq, halo_rows, stride),
        grid=(n_tiles,), in_specs=in_specs,
        out_specs=[pl.BlockSpec((TM, D_MODEL), row),
                   pl.BlockSpec((None, halo_rows, D_MODEL), lambda i: (i, 0, 0))],
        out_shape=[jax.ShapeDtypeStruct((t, D_MODEL), F32),
                   jax.ShapeDtypeStruct((n_tiles, halo_rows, D_MODEL), F32)],
        scratch_shapes=scratch,
        compiler_params=_cparams(("arbitrary",)), name="short_conv",
    )(*args)


def _rot_cols(w):
    return jnp.concatenate([-w[..., HALF_ROPE:], w[..., :HALF_ROPE]], axis=-1)


def _prepare_weights(w_a_down, g_q_norm, g_kv_norm, w_a_uq, w_a_uk, w_a_uv, w_a_out,
                     w_b_in, w_b_conv, w_b_out, w_f_up, w_f_conv, b_f_conv, w_f_down, ln_g, ln_b):
    o = Q_LORA + KV_LORA
    w_kr = w_a_down[:, o:]
    zpad = jnp.zeros((D_MODEL, LANES - QK_ROPE), F32)
    wd = jnp.concatenate([w_a_down[:, :o], w_kr, zpad, _rot_cols(w_kr), zpad], axis=1)

    wq3 = w_a_uq.reshape(Q_LORA, N_HEADS, QK_HEAD)
    zq = jnp.zeros((Q_LORA, N_HEADS, HEAD_PAD - QK_HEAD), F32)
    q_main = jnp.concatenate([wq3, zq], axis=-1)
    q_rot = jnp.concatenate([jnp.zeros((Q_LORA, N_HEADS, QK_NOPE), F32), _rot_cols(wq3[..., QK_NOPE:]), zq],
                            axis=-1)
    wq = jnp.concatenate([q_main.reshape(Q_LORA, -1), q_rot.reshape(Q_LORA, -1)], axis=1)

    wk = jnp.concatenate([w_a_uk, jnp.zeros((KV_LORA, N_HEADS, HEAD_PAD - QK_NOPE), F32)], axis=-1)
    wk = wk.reshape(KV_LORA, N_HEADS * HEAD_PAD)
    src = jnp.arange(LANES)[:, None]
    dst = jnp.arange(N_HEADS * HEAD_PAD)[None, :]
    e = ((src < QK_ROPE) & ((dst % HEAD_PAD) == QK_NOPE + src)).astype(BF16)

    w_ukt = jnp.transpose(w_a_uk, (1, 2, 0))
    w_ukt = jnp.concatenate([w_ukt, jnp.zeros((N_HEADS, HEAD_PAD - QK_NOPE, KV_LORA), F32)], axis=1)
    w_uvh = jnp.transpose(w_a_uv, (1, 0, 2))
    zv = jnp.zeros_like(w_uvh)
    even = (jnp.arange(N_HEADS) % 2 == 0)[:, None, None]
    w_uv_pair = jnp.where(even, jnp.concatenate([w_uvh, zv], -1), jnp.concatenate([zv, w_uvh], -1))

    return dict(
        wd=wd.astype(BF16), gq=g_q_norm.reshape(1, -1), gkv=g_kv_norm.reshape(1, -1), wq=wq.astype(BF16),
        wk=wk.T.astype(BF16), e=e.T, wv=w_a_uv.reshape(KV_LORA, -1).astype(BF16),
        w_ukt=w_ukt.astype(BF16), w_uv_pair=w_uv_pair.astype(BF16), w_a_out=w_a_out.astype(BF16),
        w_b_in=w_b_in.astype(BF16), w_b_conv=w_b_conv, w_b_out=w_b_out.astype(BF16),
        w_up=w_f_up.astype(BF16), w_fconv=w_f_conv, b_fconv=b_f_conv.reshape(DEPTH, 1, 2 * D_FF),
        w_down=w_f_down.astype(BF16), ln_g=ln_g.reshape(2 * DEPTH, 1, D_MODEL),
        ln_b=ln_b.reshape(2 * DEPTH, 1, D_MODEL),
    )


def _rope_tables(pos):
    inv = 1.0 / (ROPE_THETA ** (jnp.arange(0, QK_ROPE, 2, dtype=F32) / QK_ROPE))
    ang = pos.astype(F32)[:, None] * inv[None, :]
    cos, sin = jnp.cos(ang), jnp.sin(ang)
    p = pos.shape[0]
    one = jnp.ones((p, QK_NOPE), F32)
    z64 = jnp.zeros((p, QK_NOPE), F32)
    z32 = jnp.zeros((p, HEAD_PAD - QK_HEAD), F32)
    z96 = jnp.zeros((p, LANES - QK_ROPE), F32)
    cosg = jnp.concatenate([one, cos, cos, z32], axis=1)
    sing = jnp.concatenate([z64, sin, sin, z32], axis=1)
    ckr = jnp.concatenate([cos, cos, z96], axis=1)
    skr = jnp.concatenate([sin, sin, z96], axis=1)
    return cosg, sing, ckr, skr


def _prompt_trunk(x_prompt, w):
    b, s, _ = x_prompt.shape
    x = x_prompt.reshape(b * s, D_MODEL)
    tps = s // TM
    tabs = _rope_tables(jnp.arange(s))
    q, c_kv, k_r, k, v = _mla_proj(x, tabs, w, True, tps)
    o = _flash_attention(q, k, v, b, s)
    x = _out_proj_ln(x, o, w['w_a_out'], w['ln_g'][0], w['ln_b'][0])
    x, la0, lg0 = _conv_ffn(x, 0, w, tiles_per_seq=tps)
    x, lc = _short_conv(x, w, tiles_per_seq=tps)
    x, la1, lg1 = _conv_ffn(x, 1, w, tiles_per_seq=tps)

    def seq_last(a):
        return a[tps - 1::tps, HALO_PROMPT - (CONV_W - 1):, :]

    ffn_state = jnp.stack([jnp.concatenate([seq_last(la0), seq_last(lg0)], axis=-1),
                           jnp.concatenate([seq_last(la1), seq_last(lg1)], axis=-1)])
    return (x.reshape(b, s, D_MODEL),
            c_kv.reshape(b, s // PAGE_SIZE, PAGE_SIZE, KV_LORA),
            k_r.reshape(b, s // PAGE_SIZE, PAGE_SIZE, QK_ROPE),
            seq_last(lc), ffn_state)


def _sample_trunk(x_sample, cache_c, cache_r, state_conv_b, state_ffn_conv, page_table, w):
    b, t, _ = x_sample.shape
    n = b * t
    past_len = page_table.shape[1] * cache_c.shape[1]
    tm_rows = lambda a: jnp.swapaxes(a, 0, 1).reshape(a.shape[1] * a.shape[0], a.shape[-1])
    bm_rows = lambda a, k: jnp.swapaxes(a.reshape(k, b, a.shape[-1]), 0, 1)
    x = tm_rows(x_sample)
    pos = past_len + jnp.repeat(jnp.arange(t), b)
    tabs = _rope_tables(pos)
    q, c_kv, k_r = _mla_proj(x, tabs, w, False, n // TM)

    q_lat = _absorb_q(q, w['w_ukt'])
    q_lat = jnp.transpose(q_lat.reshape(N_HEADS, t, b, KV_LORA), (2, 1, 0, 3)).reshape(b, t * N_HEADS, KV_LORA)
    q_rope = q.reshape(t, b, N_HEADS, HEAD_PAD)[..., QK_NOPE:QK_HEAD]
    q_rope = jnp.transpose(q_rope, (1, 0, 2, 3)).reshape(b, t * N_HEADS, QK_ROPE)
    pad_new = lambda a: jnp.pad(bm_rows(a, t), ((0, 0), (0, SUBLANES - t), (0, 0)))
    cache_rt = jnp.swapaxes(cache_r, 1, 2)
    o_lat = _decode_attention(q_lat, q_rope, pad_new(c_kv), pad_new(k_r), cache_c, cache_rt, page_table, t)
    o_lat = jnp.transpose(o_lat.reshape(b, t, N_HEADS, KV_LORA), (2, 1, 0, 3)).reshape(N_HEADS, n, KV_LORA)
    o = _unabsorb_o(o_lat.astype(BF16), w['w_uv_pair'])

    x = _out_proj_ln(x, o, w['w_a_out'], w['ln_g'][0], w['ln_b'][0])
    x, la0, lg0 = _conv_ffn(x, 0, w, state=tm_rows(state_ffn_conv[0]), stride=b)
    x, lc = _short_conv(x, w, state=tm_rows(state_conv_b), stride=b)
    x, la1, lg1 = _conv_ffn(x, 1, w, state=tm_rows(state_ffn_conv[1]), stride=b)

    k = CONV_W - 1
    ffn_state = jnp.stack([bm_rows(jnp.concatenate([la0[0], lg0[0]], axis=-1), k),
                           bm_rows(jnp.concatenate([la1[0], lg1[0]], axis=-1), k)])
    return (bm_rows(x, t), bm_rows(c_kv, t), bm_rows(k_r, t), bm_rows(lc[0], k), ffn_state)


def kernel(x_prompt, x_sample, cache_kv_latent, cache_k_rope, state_conv_b, state_ffn_conv, page_table,
           w_a_down, g_q_norm, g_kv_norm, w_a_uq, w_a_uk, w_a_uv, w_a_out,
           w_b_in, w_b_conv, w_b_out, w_f_up, w_f_conv, b_f_conv, w_f_down, ln_g, ln_b):
    w = _prepare_weights(w_a_down, g_q_norm, g_kv_norm, w_a_uq, w_a_uk, w_a_uv, w_a_out,
                         w_b_in, w_b_conv, w_b_out, w_f_up, w_f_conv, b_f_conv, w_f_down, ln_g, ln_b)
    y_p, lat_p, rope_p, conv_p, ffn_p = _prompt_trunk(x_prompt, w)
    y_s, lat_s, rope_s, conv_s, ffn_s = _sample_trunk(
        x_sample, cache_kv_latent, cache_k_rope, state_conv_b, state_ffn_conv, page_table, w)
    return (y_p, y_s, lat_p, rope_p, lat_s, rope_s, conv_p, conv_s, ffn_p, ffn_s)
```

```python
import functools
import math

import jax
import jax.numpy as jnp
from jax import lax
from jax.experimental import pallas as pl
from jax.experimental.pallas import tpu as pltpu

D_MODEL = 1024
N_HEADS = 16
Q_LORA = 384
KV_LORA = 256
QK_NOPE = 64
QK_ROPE = 32
HALF_ROPE = QK_ROPE // 2
V_HEAD = 64
QK_HEAD = QK_NOPE + QK_ROPE
ROPE_THETA = 10000.0
ATTN_SCALE = QK_HEAD ** -0.5
Q_SCALE = ATTN_SCALE * math.log2(math.e)
CONV_W = 3
D_FF = 2816
RMS_EPS = 1e-6
LN_EPS = 1e-5
DEPTH = 2
DEEPNORM_ALPHA = (2 * DEPTH) ** 0.25
PAGE_SIZE = 128

LANES = 128
SUBLANES = 8
HEAD_PAD = LANES
VMEM_LIMIT = 56 * 1024 * 1024

TM = 512
TQ = 512
TK = 512
RQ = 128
QK_AHEAD = 4
FF_CHUNK = 1408
N_FF_CHUNKS = D_FF // FF_CHUNK
FF_ROWS = 128
ROW_BLOCK = 128
MLA_ROWS = 256
UP_AHEAD = 1
PAGES_PER_GROUP = 16
DECODE_SLOTS = 4
HALO_PROMPT = SUBLANES

BF16 = jnp.bfloat16
F32 = jnp.float32


def _dot(a, b):
    return jnp.dot(a, b, preferred_element_type=F32)


def _dot_nt(a, b):
    return lax.dot_general(a, b, (((1,), (1,)), ((), ())), preferred_element_type=F32)


def _layer_norm(v, g, b):
    mu = jnp.mean(v, axis=-1, keepdims=True)
    vc = v - mu
    var = jnp.mean(vc * vc, axis=-1, keepdims=True)
    return vc * lax.rsqrt(var + LN_EPS) * g + b


def _rms_norm(v, g):
    return v * lax.rsqrt(jnp.mean(v * v, axis=-1, keepdims=True) + RMS_EPS) * g


def _software_pipeline(n_blocks, ahead, first_stage, second_stage):
    pending = [first_stage(r) for r in range(min(ahead, n_blocks))]
    for r in range(n_blocks):
        value = pending.pop(0)
        if r + ahead < n_blocks:
            pending.append(first_stage(r + ahead))
        second_stage(r, value)


def _cparams(sem):
    return pltpu.CompilerParams(dimension_semantics=sem, vmem_limit_bytes=VMEM_LIMIT)


def _mla_proj_kernel(emit_kv, x_ref, wd_ref, gq_ref, gkv_ref, wq_ref, cosg_ref, sing_ref,
                     ckr_ref, skr_ref, *rest):
    if emit_kv:
        wk_ref, e_ref, wv_ref, q_ref, ckv_ref, kr_ref, k_ref, v_ref = rest
    else:
        q_ref, ckv_ref, kr_ref = rest
    qw = N_HEADS * HEAD_PAD
    o = Q_LORA + KV_LORA

    def down(r):
        rows = slice(r * MLA_ROWS, (r + 1) * MLA_ROWS)
        return _dot(x_ref[rows, :].astype(BF16), wd_ref[...])

    def rest_of(r, d):
        rows = slice(r * MLA_ROWS, (r + 1) * MLA_ROWS)
        c_q = _rms_norm(d[:, :Q_LORA], gq_ref[...])
        c_kv = _rms_norm(d[:, Q_LORA:o], gkv_ref[...])
        ckv_ref[rows, :] = c_kv
        krg = d[:, o:o + LANES] * ckr_ref[rows, :] + d[:, o + LANES:o + 2 * LANES] * skr_ref[rows, :]
        kr_ref[rows, :] = krg[:, :QK_ROPE]
        q2 = _dot(c_q.astype(BF16), wq_ref[...])
        cosg = cosg_ref[rows, :] * Q_SCALE
        sing = sing_ref[rows, :] * Q_SCALE
        for h in range(N_HEADS):
            lo = h * HEAD_PAD
            q_ref[rows, lo:lo + HEAD_PAD] = (q2[:, lo:lo + HEAD_PAD] * cosg
                                             + q2[:, qw + lo:qw + lo + HEAD_PAD] * sing).astype(BF16)
        if emit_kv:
            ckvb = c_kv.astype(BF16)
            kt = _dot_nt(wk_ref[...], ckvb) + _dot_nt(e_ref[...], krg.astype(BF16))
            k_ref[:, rows] = kt.astype(BF16)
            v_ref[rows, :] = _dot(ckvb, wv_ref[...]).astype(BF16)

    _software_pipeline(x_ref.shape[0] // MLA_ROWS, 1, down, rest_of)


def _mla_proj(x, tabs, w, emit_kv, pos_tiles):
    t = x.shape[0]
    n = t // TM
    qw = N_HEADS * HEAD_PAD
    const = lambda i: (0, 0)
    row = lambda i: (i, 0)
    tab = lambda i: (i % pos_tiles, 0)
    in_specs = [
        pl.BlockSpec((TM, D_MODEL), row),
        pl.BlockSpec(w['wd'].shape, const),
        pl.BlockSpec((1, Q_LORA), const),
        pl.BlockSpec((1, KV_LORA), const),
        pl.BlockSpec(w['wq'].shape, const),
    ] + [pl.BlockSpec((TM, LANES), tab)] * 4
    args = [x, w['wd'], w['gq'], w['gkv'], w['wq']] + list(tabs)
    out_shape = [jax.ShapeDtypeStruct((t, qw), BF16),
                 jax.ShapeDtypeStruct((t, KV_LORA), F32),
                 jax.ShapeDtypeStruct((t, QK_ROPE), F32)]
    out_specs = [pl.BlockSpec((TM, qw), row), pl.BlockSpec((TM, KV_LORA), row),
                 pl.BlockSpec((TM, QK_ROPE), row)]
    if emit_kv:
        in_specs += [pl.BlockSpec(w['wk'].shape, const), pl.BlockSpec(w['e'].shape, const),
                     pl.BlockSpec(w['wv'].shape, const)]
        args += [w['wk'], w['e'], w['wv']]
        out_shape += [jax.ShapeDtypeStruct((qw, t), BF16),
                      jax.ShapeDtypeStruct((t, N_HEADS * V_HEAD), BF16)]
        out_specs += [pl.BlockSpec((qw, TM), lambda i: (0, i)),
                      pl.BlockSpec((TM, N_HEADS * V_HEAD), row)]
    return pl.pallas_call(
        functools.partial(_mla_proj_kernel, emit_kv),
        grid=(n,), in_specs=in_specs, out_specs=out_specs, out_shape=out_shape,
        compiler_params=_cparams(("parallel",)), name="mla_proj_kv" if emit_kv else "mla_proj",
    )(*args)


def _flash_kernel(qi_tab, ki_tab, q_ref, k_ref, v_ref, o_ref, m_sc, l_sc, acc_sc):
    step = pl.program_id(2)
    qi = qi_tab[step]
    ki = ki_tab[step]

    @pl.when(ki == 0)
    def _():
        m_sc[...] = jnp.full(m_sc.shape, -jnp.inf, F32)
        l_sc[...] = jnp.zeros(l_sc.shape, F32)
        acc_sc[...] = jnp.zeros(acc_sc.shape, F32)

    def tile(diagonal):
        chains = [(hh, r) for hh in range(2) for r in range(TQ // RQ)]

        def n_keys(r):
            return (r + 1) * RQ if diagonal else TK

        def scores(chain):
            hh, r = chain
            hs = slice(hh * HEAD_PAD, (hh + 1) * HEAD_PAD)
            return _dot(q_ref[r * RQ:(r + 1) * RQ, hs], k_ref[hs, 0:n_keys(r)])

        def finish(chain, s):
            hh, r = chain
            rows = slice(r * RQ, (r + 1) * RQ)
            nk = n_keys(r)
            if diagonal:
                row = lax.broadcasted_iota(jnp.int32, (RQ, nk), 0) + r * RQ
                col = lax.broadcasted_iota(jnp.int32, (RQ, nk), 1)
                s = jnp.where(col <= row, s, -jnp.inf)
            chunks = [s[:, c * LANES:(c + 1) * LANES] for c in range(nk // LANES)]
            mx = chunks[0]
            for c in chunks[1:]:
                mx = jnp.maximum(mx, c)
            m_prev = m_sc[hh, rows, :]
            m_new = jnp.maximum(m_prev, jnp.max(mx, axis=-1, keepdims=True))
            alpha = jnp.exp2(m_prev - m_new)
            ps = [jnp.exp2(c - m_new) for c in chunks]
            psum = ps[0]
            for pc in ps[1:]:
                psum = psum + pc
            p = jnp.concatenate(ps, axis=1).astype(BF16) if len(ps) > 1 else ps[0].astype(BF16)
            l_sc[hh, rows, :] = alpha * l_sc[hh, rows, :] + psum
            acc_sc[hh, rows, :] = alpha * acc_sc[hh, rows, :] + _dot(p, v_ref[0:nk, :])
            m_sc[hh, rows, :] = m_new

        _software_pipeline(len(chains), QK_AHEAD, lambda i: scores(chains[i]),
                           lambda i, s: finish(chains[i], s))

    @pl.when(ki < qi)
    def _():
        tile(False)

    @pl.when(ki == qi)
    def _():
        tile(True)
        lane = lax.broadcasted_iota(jnp.int32, (TQ, 2 * V_HEAD), 1)
        o0 = acc_sc[0] / jnp.sum(l_sc[0], axis=-1, keepdims=True)
        o1 = acc_sc[1] / jnp.sum(l_sc[1], axis=-1, keepdims=True)
        o_ref[...] = jnp.where(lane < V_HEAD, o0, o1).astype(BF16)


def _flash_attention(q, k, v, batch, seq):
    nq = seq // TQ
    pairs = [(a, b) for a in range(nq) for b in range(a + 1)]
    qi_tab = jnp.asarray([p[0] for p in pairs], jnp.int32)
    ki_tab = jnp.asarray([p[1] for p in pairs], jnp.int32)
    grid_spec = pltpu.PrefetchScalarGridSpec(
        num_scalar_prefetch=2,
        grid=(batch, N_HEADS // 2, len(pairs)),
        in_specs=[
            pl.BlockSpec((TQ, 2 * HEAD_PAD), lambda b, h, s, qt, kt: (b * nq + qt[s], h)),
            pl.BlockSpec((2 * HEAD_PAD, TK), lambda b, h, s, qt, kt: (h, b * nq + kt[s])),
            pl.BlockSpec((TK, 2 * V_HEAD), lambda b, h, s, qt, kt: (b * nq + kt[s], h)),
        ],
        out_specs=pl.BlockSpec((TQ, 2 * V_HEAD), lambda b, h, s, qt, kt: (b * nq + qt[s], h)),
        scratch_shapes=[pltpu.VMEM((2, TQ, LANES), F32), pltpu.VMEM((2, TQ, LANES), F32),
                        pltpu.VMEM((2, TQ, 2 * V_HEAD), F32)],
    )
    return pl.pallas_call(
        _flash_kernel, grid_spec=grid_spec,
        out_shape=jax.ShapeDtypeStruct((batch * seq, N_HEADS * V_HEAD), BF16),
        compiler_params=_cparams(("parallel", "parallel", "arbitrary")), name="flash_attn",
    )(qi_tab, ki_tab, q, k, v)


def _absorb_kernel(q_ref, w_ref, o_ref):
    o_ref[...] = _dot(q_ref[...], w_ref[...]).astype(BF16)


def _absorb_q(q, w_ukt):
    t = q.shape[0]
    return pl.pallas_call(
        _absorb_kernel, grid=(N_HEADS,),
        in_specs=[pl.BlockSpec((t, HEAD_PAD), lambda h: (0, h)),
                  pl.BlockSpec((None, HEAD_PAD, KV_LORA), lambda h: (h, 0, 0))],
        out_specs=pl.BlockSpec((None, t, KV_LORA), lambda h: (h, 0, 0)),
        out_shape=jax.ShapeDtypeStruct((N_HEADS, t, KV_LORA), BF16),
        compiler_params=_cparams(("parallel",)), name="absorb_q",
    )(q, w_ukt)


def _unabsorb_kernel(o_ref, w_ref, out_ref):
    out_ref[...] = (_dot(o_ref[0], w_ref[0]) + _dot(o_ref[1], w_ref[1])).astype(BF16)


def _unabsorb_o(o_lat, w_uv_pair):
    t = o_lat.shape[1]
    return pl.pallas_call(
        _unabsorb_kernel, grid=(N_HEADS // 2,),
        in_specs=[pl.BlockSpec((2, t, KV_LORA), lambda j: (j, 0, 0)),
                  pl.BlockSpec((2, KV_LORA, 2 * V_HEAD), lambda j: (j, 0, 0))],
        out_specs=pl.BlockSpec((t, 2 * V_HEAD), lambda j: (0, j)),
        out_shape=jax.ShapeDtypeStruct((t, N_HEADS * V_HEAD), BF16),
        compiler_params=_cparams(("parallel",)), name="unabsorb_o",
    )(o_lat, w_uv_pair)


def _decode_kernel(n_new, n_groups, pt_ref, ql_ref, qr_ref, cn_ref, rn_ref, cache_c, cache_r, o_ref,
                   cbuf, rbuf, sem_c, sem_r):
    b = pl.program_id(0)
    rows = ql_ref.shape[0]
    gk = PAGES_PER_GROUP * PAGE_SIZE

    def group_copies(seq, g):
        slot = g % DECODE_SLOTS
        copies = []
        for j in range(PAGES_PER_GROUP):
            page = pt_ref[seq, g * PAGES_PER_GROUP + j]
            keys = pl.ds(j * PAGE_SIZE, PAGE_SIZE)
            copies.append(pltpu.make_async_copy(cache_c.at[page], cbuf.at[slot, keys, :], sem_c.at[slot]))
            copies.append(pltpu.make_async_copy(cache_r.at[page], rbuf.at[slot, :, keys], sem_r.at[slot]))
        return copies

    def start(seq, g):
        for cp in group_copies(seq, g):
            cp.start()

    def wait(seq, g):
        for cp in group_copies(seq, g):
            cp.wait()

    @pl.when(b == 0)
    def _():
        for g in range(DECODE_SLOTS):
            start(0, g)

    ql = ql_ref[...]
    qr = qr_ref[...]

    def scores(g):
        slot = g % DECODE_SLOTS
        c = cbuf[slot].astype(BF16)
        s = _dot_nt(ql, c) + _dot(qr, rbuf[slot].astype(BF16))
        return s, c

    m = jnp.full((rows, LANES), -jnp.inf, F32)
    l = jnp.zeros((rows, LANES), F32)
    acc = jnp.zeros((rows, KV_LORA), F32)
    wait(b, 0)
    nxt = scores(0)
    for g in range(n_groups):
        s, c = nxt
        if g + 1 < n_groups:
            wait(b, g + 1)
            nxt = scores(g + 1)
        chunks = [s[:, k * LANES:(k + 1) * LANES] for k in range(gk // LANES)]
        mx = chunks[0]
        for ch in chunks[1:]:
            mx = jnp.maximum(mx, ch)
        m_new = jnp.maximum(m, jnp.max(mx, axis=-1, keepdims=True))
        alpha = jnp.exp2(m - m_new)
        ps = [jnp.exp2(ch - m_new) for ch in chunks]
        psum = ps[0]
        for pc in ps[1:]:
            psum = psum + pc
        l = alpha * l + psum
        acc = alpha[:, 0:1] * acc + _dot(jnp.concatenate(ps, axis=1).astype(BF16), c)
        m = m_new
        if g + DECODE_SLOTS < n_groups:
            start(b, g + DECODE_SLOTS)
        else:
            @pl.when(b + 1 < pl.num_programs(0))
            def _():
                start(b + 1, g + DECODE_SLOTS - n_groups)

    qlf = ql.astype(F32)
    qrf = qr.astype(F32)
    row = lax.broadcasted_iota(jnp.int32, (rows, 1), 0)
    s_new = []
    for j in range(n_new):
        sj = (jnp.sum(qlf * cn_ref[j:j + 1, :], axis=-1, keepdims=True)
              + jnp.sum(qrf * rn_ref[j:j + 1, :], axis=-1, keepdims=True))
        s_new.append(jnp.where(row >= j * N_HEADS, sj, -jnp.inf))
    m_old = m[:, 0:1]
    m_fin = m_old
    for sj in s_new:
        m_fin = jnp.maximum(m_fin, sj)
    a = jnp.exp2(m_old - m_fin)
    l_fin = a * jnp.sum(l, axis=-1, keepdims=True)
    acc = a * acc
    for j in range(n_new):
        pj = jnp.exp2(s_new[j] - m_fin)
        l_fin = l_fin + pj
        acc = acc + pj * cn_ref[j:j + 1, :]
    o_ref[...] = acc / l_fin


def _decode_attention(q_lat, q_rope, c_new, r_new, cache_c, cache_r, page_table, n_new):
    b, rows, _ = q_lat.shape
    n_pages = page_table.shape[1]
    assert n_pages % PAGES_PER_GROUP == 0
    n_groups = n_pages // PAGES_PER_GROUP
    assert n_groups % DECODE_SLOTS == 0
    gk = PAGES_PER_GROUP * PAGE_SIZE
    seq = lambda i, pt: (i, 0, 0)
    any_spec = pl.BlockSpec(memory_space=pl.ANY)
    grid_spec = pltpu.PrefetchScalarGridSpec(
        num_scalar_prefetch=1, grid=(b,),
        in_specs=[pl.BlockSpec((None, rows, KV_LORA), seq), pl.BlockSpec((None, rows, QK_ROPE), seq),
                  pl.BlockSpec((None, SUBLANES, KV_LORA), seq), pl.BlockSpec((None, SUBLANES, QK_ROPE), seq),
                  any_spec, any_spec],
        out_specs=pl.BlockSpec((None, rows, KV_LORA), seq),
        scratch_shapes=[pltpu.VMEM((DECODE_SLOTS, gk, KV_LORA), F32),
                        pltpu.VMEM((DECODE_SLOTS, QK_ROPE, gk), F32),
                        pltpu.SemaphoreType.DMA((DECODE_SLOTS,)),
                        pltpu.SemaphoreType.DMA((DECODE_SLOTS,))],
    )
    return pl.pallas_call(
        functools.partial(_decode_kernel, n_new, n_groups), grid_spec=grid_spec,
        out_shape=jax.ShapeDtypeStruct((b, rows, KV_LORA), F32),
        compiler_params=_cparams(("arbitrary",)), name="decode_attn",
    )(page_table, q_lat, q_rope, c_new, r_new, cache_c, cache_r)


def _out_proj_kernel(x_ref, o_ref, w_ref, g_ref, b_ref, y_ref):
    def proj(r):
        return _dot(o_ref[r * ROW_BLOCK:(r + 1) * ROW_BLOCK, :], w_ref[...])

    def norm(r, mix):
        rows = slice(r * ROW_BLOCK, (r + 1) * ROW_BLOCK)
        y_ref[rows, :] = _layer_norm(DEEPNORM_ALPHA * x_ref[rows, :] + mix, g_ref[...], b_ref[...])

    _software_pipeline(x_ref.shape[0] // ROW_BLOCK, 1, proj, norm)


def _out_proj_ln(x, o, w, g, b):
    t = x.shape[0]
    const = lambda i: (0, 0)
    row = lambda i: (i, 0)
    return pl.pallas_call(
        _out_proj_kernel, grid=(t // TM,),
        in_specs=[pl.BlockSpec((TM, D_MODEL), row), pl.BlockSpec((TM, N_HEADS * V_HEAD), row),
                  pl.BlockSpec(w.shape, const), pl.BlockSpec((1, D_MODEL), const),
                  pl.BlockSpec((1, D_MODEL), const)],
        out_specs=pl.BlockSpec((TM, D_MODEL), row),
        out_shape=jax.ShapeDtypeStruct((t, D_MODEL), F32),
        compiler_params=_cparams(("parallel",)), name="attn_out_ln",
    )(x, o, w, g, b)


def _causal_conv(buf, u, row0, halo_rows, stride, w_ref):
    n = u.shape[0]
    base = halo_rows + row0
    buf[base:base + n, :] = u
    y = w_ref[CONV_W - 1:CONV_W, :] * u
    for j in range(CONV_W - 1):
        back = (CONV_W - 1 - j) * stride
        y = y + w_ref[j:j + 1, :] * buf[base - back:base - back + n, :]
    return y


def _ffn_kernel(tiles_per_seq, halo_rows, stride, x_ref, wua_ref, wug_ref, wca_ref, wcg_ref,
                ba_ref, bg_ref, wd_ref, g_ref, b_ref, *rest):
    sample = tiles_per_seq is None
    if sample:
        sa_ref, sg_ref, y_ref, la_ref, lg_ref, buf_a, buf_g, acc = rest
    else:
        y_ref, la_ref, lg_ref, buf_a, buf_g, acc, car_a, car_g = rest
    i = pl.program_id(0)
    j = pl.program_id(1)
    n = x_ref.shape[0]
    if sample:
        buf_a[0:halo_rows, :] = sa_ref[...]
        buf_g[0:halo_rows, :] = sg_ref[...]
    else:
        first = (i % tiles_per_seq) == 0

        @pl.when(first)
        def _():
            buf_a[0:halo_rows, :] = jnp.zeros((halo_rows, FF_CHUNK), F32)
            buf_g[0:halo_rows, :] = jnp.zeros((halo_rows, FF_CHUNK), F32)

        @pl.when(jnp.logical_not(first))
        def _():
            buf_a[0:halo_rows, :] = car_a[j]
            buf_g[0:halo_rows, :] = car_g[j]

    @pl.when(j == 0)
    def _():
        acc[...] = jnp.zeros(acc.shape, F32)

    def up(r):
        xb = x_ref[r * FF_ROWS:(r + 1) * FF_ROWS, :].astype(BF16)
        return _dot(xb, wua_ref[...]), _dot(xb, wug_ref[...])

    def down(r, h):
        ha, hg = h
        rows = slice(r * FF_ROWS, (r + 1) * FF_ROWS)
        ya = _causal_conv(buf_a, ha, r * FF_ROWS, halo_rows, stride, wca_ref) + ba_ref[...]
        yg = _causal_conv(buf_g, hg, r * FF_ROWS, halo_rows, stride, wcg_ref) + bg_ref[...]
        act = (yg * jax.nn.sigmoid(yg) * ya).astype(BF16)
        acc[rows, :] = acc[rows, :] + _dot(act, wd_ref[...])

    _software_pipeline(n // FF_ROWS, UP_AHEAD, up, down)

    last_a = buf_a[n:n + halo_rows, :]
    last_g = buf_g[n:n + halo_rows, :]
    la_ref[...] = last_a
    lg_ref[...] = last_g
    if not sample:
        car_a[j] = last_a
        car_g[j] = last_g

    @pl.when(j == pl.num_programs(1) - 1)
    def _():
        y_ref[...] = _layer_norm(DEEPNORM_ALPHA * x_ref[...] + acc[...], g_ref[...], b_ref[...])


def _conv_ffn(x, layer, w, state=None, tiles_per_seq=None, stride=1):
    t = x.shape[0]
    n_tiles = t // TM
    sample = state is not None
    halo_rows = (CONV_W - 1) * stride if sample else HALO_PROMPT
    nc = N_FF_CHUNKS
    in_specs = [
        pl.BlockSpec((TM, D_MODEL), lambda i, j: (i, 0)),
        pl.BlockSpec((None, D_MODEL, FF_CHUNK), lambda i, j: (layer, 0, j)),
        pl.BlockSpec((None, D_MODEL, FF_CHUNK), lambda i, j: (layer, 0, nc + j)),
        pl.BlockSpec((None, CONV_W, FF_CHUNK), lambda i, j: (layer, 0, j)),
        pl.BlockSpec((None, CONV_W, FF_CHUNK), lambda i, j: (layer, 0, nc + j)),
        pl.BlockSpec((None, 1, FF_CHUNK), lambda i, j: (layer, 0, j)),
        pl.BlockSpec((None, 1, FF_CHUNK), lambda i, j: (layer, 0, nc + j)),
        pl.BlockSpec((None, FF_CHUNK, D_MODEL), lambda i, j: (layer, j, 0)),
        pl.BlockSpec((None, 1, D_MODEL), lambda i, j: (2 * layer + 1, 0, 0)),
        pl.BlockSpec((None, 1, D_MODEL), lambda i, j: (2 * layer + 1, 0, 0)),
    ]
    args = [x, w['w_up'], w['w_up'], w['w_fconv'], w['w_fconv'], w['b_fconv'], w['b_fconv'],
            w['w_down'], w['ln_g'], w['ln_b']]
    scratch = [pltpu.VMEM((halo_rows + TM, FF_CHUNK), F32), pltpu.VMEM((halo_rows + TM, FF_CHUNK), F32),
               pltpu.VMEM((TM, D_MODEL), F32)]
    if sample:
        in_specs += [pl.BlockSpec((halo_rows, FF_CHUNK), lambda i, j: (0, j)),
                     pl.BlockSpec((halo_rows, FF_CHUNK), lambda i, j: (0, nc + j))]
        args += [state, state]
    else:
        scratch += [pltpu.VMEM((nc, halo_rows, FF_CHUNK), F32), pltpu.VMEM((nc, halo_rows, FF_CHUNK), F32)]
    out_shape = [jax.ShapeDtypeStruct((t, D_MODEL), F32),
                 jax.ShapeDtypeStruct((n_tiles, halo_rows, D_FF), F32),
                 jax.ShapeDtypeStruct((n_tiles, halo_rows, D_FF), F32)]
    out_specs = [pl.BlockSpec((TM, D_MODEL), lambda i, j: (i, 0)),
                 pl.BlockSpec((None, halo_rows, FF_CHUNK), lambda i, j: (i, 0, j)),
                 pl.BlockSpec((None, halo_rows, FF_CHUNK), lambda i, j: (i, 0, j))]
    return pl.pallas_call(
        functools.partial(_ffn_kernel, None if sample else tiles_per_seq, halo_rows, stride),
        grid=(n_tiles, nc), in_specs=in_specs, out_specs=out_specs, out_shape=out_shape,
        scratch_shapes=scratch,
        compiler_params=_cparams(("arbitrary", "arbitrary")), name="conv_ffn",
    )(*args)


def _sconv_kernel(tiles_per_seq, halo_rows, stride, x_ref, win_ref, wc_ref, wout_ref, g_ref, b_ref,
                  *rest):
    sample = tiles_per_seq is None
    if sample:
        s_ref, y_ref, last_ref, buf = rest
    else:
        y_ref, last_ref, buf, car = rest
    i = pl.program_id(0)
    n = x_ref.shape[0]
    if sample:
        buf[0:halo_rows, :] = s_ref[...]
    else:
        first = (i % tiles_per_seq) == 0

        @pl.when(first)
        def _():
            buf[0:halo_rows, :] = jnp.zeros((halo_rows, D_MODEL), F32)

        @pl.when(jnp.logical_not(first))
        def _():
            buf[0:halo_rows, :] = car[...]

    def proj_in(r):
        return _dot(x_ref[r * ROW_BLOCK:(r + 1) * ROW_BLOCK, :].astype(BF16), win_ref[...])

    def mix_out(r, bch):
        rows = slice(r * ROW_BLOCK, (r + 1) * ROW_BLOCK)
        g_b = bch[:, :D_MODEL]
        u = bch[:, D_MODEL:2 * D_MODEL] * bch[:, 2 * D_MODEL:]
        y = _causal_conv(buf, u, r * ROW_BLOCK, halo_rows, stride, wc_ref)
        mix = _dot((g_b * y).astype(BF16), wout_ref[...])
        y_ref[rows, :] = _layer_norm(DEEPNORM_ALPHA * x_ref[rows, :] + mix, g_ref[...], b_ref[...])

    _software_pipeline(n // ROW_BLOCK, 1, proj_in, mix_out)

    last = buf[n:n + halo_rows, :]
    last_ref[...] = last
    if not sample:
        car[...] = last


def _short_conv(x, w, state=None, tiles_per_seq=None, stride=1):
    t = x.shape[0]
    n_tiles = t // TM
    sample = state is not None
    halo_rows = (CONV_W - 1) * stride if sample else HALO_PROMPT
    const = lambda i: (0, 0)
    row = lambda i: (i, 0)
    in_specs = [pl.BlockSpec((TM, D_MODEL), row), pl.BlockSpec(w['w_b_in'].shape, const),
                pl.BlockSpec((CONV_W, D_MODEL), const), pl.BlockSpec(w['w_b_out'].shape, const),
                pl.BlockSpec((None, 1, D_MODEL), lambda i: (2, 0, 0)),
                pl.BlockSpec((None, 1, D_MODEL), lambda i: (2, 0, 0))]
    args = [x, w['w_b_in'], w['w_b_conv'], w['w_b_out'], w['ln_g'], w['ln_b']]
    scratch = [pltpu.VMEM((halo_rows + TM, D_MODEL), F32)]
    if sample:
        in_specs += [pl.BlockSpec((halo_rows, D_MODEL), const)]
        args += [state]
    else:
        scratch += [pltpu.VMEM((halo_rows, D_MODEL), F32)]
    return pl.pallas_call(
        functools.partial(_sconv_kernel, None if sample else tiles_per_seq, halo_rows, stride),
        grid=(n_tiles,), in_specs=in_specs,
        out_specs=[pl.BlockSpec((TM, D_MODEL), row),
                   pl.BlockSpec((None, halo_rows, D_MODEL), lambda i: (i, 0, 0))],
        out_shape=[jax.ShapeDtypeStruct((t, D_MODEL), F32),
                   jax.ShapeDtypeStruct((n_tiles, halo_rows, D_MODEL), F32)],
        scratch_shapes=scratch,
        compiler_params=_cparams(("arbitrary",)), name="short_conv",
    )(*args)


def _rot_cols(w):
    return jnp.concatenate([-w[..., HALF_ROPE:], w[..., :HALF_ROPE]], axis=-1)


def _prepare_weights(w_a_down, g_q_norm, g_kv_norm, w_a_uq, w_a_uk, w_a_uv, w_a_out,
                     w_b_in, w_b_conv, w_b_out, w_f_up, w_f_conv, b_f_conv, w_f_down, ln_g, ln_b):
    o = Q_LORA + KV_LORA
    w_kr = w_a_down[:, o:]
    zpad = jnp.zeros((D_MODEL, LANES - QK_ROPE), F32)
    wd = jnp.concatenate([w_a_down[:, :o], w_kr, zpad, _rot_cols(w_kr), zpad], axis=1)

    wq3 = w_a_uq.reshape(Q_LORA, N_HEADS, QK_HEAD)
    zq = jnp.zeros((Q_LORA, N_HEADS, HEAD_PAD - QK_HEAD), F32)
    q_main = jnp.concatenate([wq3, zq], axis=-1)
    q_rot = jnp.concatenate([jnp.zeros((Q_LORA, N_HEADS, QK_NOPE), F32), _rot_cols(wq3[..., QK_NOPE:]), zq],
                            axis=-1)
    wq = jnp.concatenate([q_main.reshape(Q_LORA, -1), q_rot.reshape(Q_LORA, -1)], axis=1)

    wk = jnp.concatenate([w_a_uk, jnp.zeros((KV_LORA, N_HEADS, HEAD_PAD - QK_NOPE), F32)], axis=-1)
    wk = wk.reshape(KV_LORA, N_HEADS * HEAD_PAD)
    src = jnp.arange(LANES)[:, None]
    dst = jnp.arange(N_HEADS * HEAD_PAD)[None, :]
    e = ((src < QK_ROPE) & ((dst % HEAD_PAD) == QK_NOPE + src)).astype(BF16)

    w_ukt = jnp.transpose(w_a_uk, (1, 2, 0))
    w_ukt = jnp.concatenate([w_ukt, jnp.zeros((N_HEADS, HEAD_PAD - QK_NOPE, KV_LORA), F32)], axis=1)
    w_uvh = jnp.transpose(w_a_uv, (1, 0, 2))
    zv = jnp.zeros_like(w_uvh)
    even = (jnp.arange(N_HEADS) % 2 == 0)[:, None, None]
    w_uv_pair = jnp.where(even, jnp.concatenate([w_uvh, zv], -1), jnp.concatenate([zv, w_uvh], -1))

    return dict(
        wd=wd.astype(BF16), gq=g_q_norm.reshape(1, -1), gkv=g_kv_norm.reshape(1, -1), wq=wq.astype(BF16),
        wk=wk.T.astype(BF16), e=e.T, wv=w_a_uv.reshape(KV_LORA, -1).astype(BF16),
        w_ukt=w_ukt.astype(BF16), w_uv_pair=w_uv_pair.astype(BF16), w_a_out=w_a_out.astype(BF16),
        w_b_in=w_b_in.astype(BF16), w_b_conv=w_b_conv, w_b_out=w_b_out.astype(BF16),
        w_up=w_f_up.astype(BF16), w_fconv=w_f_conv, b_fconv=b_f_conv.reshape(DEPTH, 1, 2 * D_FF),
        w_down=w_f_down.astype(BF16), ln_g=ln_g.reshape(2 * DEPTH, 1, D_MODEL),
        ln_b=ln_b.reshape(2 * DEPTH, 1, D_MODEL),
    )


def _rope_tables(pos):
    inv = 1.0 / (ROPE_THETA ** (jnp.arange(0, QK_ROPE, 2, dtype=F32) / QK_ROPE))
    ang = pos.astype(F32)[:, None] * inv[None, :]
    cos, sin = jnp.cos(ang), jnp.sin(ang)
    p = pos.shape[0]
    one = jnp.ones((p, QK_NOPE), F32)
    z64 = jnp.zeros((p, QK_NOPE), F32)
    z32 = jnp.zeros((p, HEAD_PAD - QK_HEAD), F32)
    z96 = jnp.zeros((p, LANES - QK_ROPE), F32)
    cosg = jnp.concatenate([one, cos, cos, z32], axis=1)
    sing = jnp.concatenate([z64, sin, sin, z32], axis=1)
    ckr = jnp.concatenate([cos, cos, z96], axis=1)
    skr = jnp.concatenate([sin, sin, z96], axis=1)
    return cosg, sing, ckr, skr


def _prompt_trunk(x_prompt, w):
    b, s, _ = x_prompt.shape
    x = x_prompt.reshape(b * s, D_MODEL)
    tps = s // TM
    tabs = _rope_tables(jnp.arange(s))
    q, c_kv, k_r, k, v = _mla_proj(x, tabs, w, True, tps)
    o = _flash_attention(q, k, v, b, s)
    x = _out_proj_ln(x, o, w['w_a_out'], w['ln_g'][0], w['ln_b'][0])
    x, la0, lg0 = _conv_ffn(x, 0, w, tiles_per_seq=tps)
    x, lc = _short_conv(x, w, tiles_per_seq=tps)
    x, la1, lg1 = _conv_ffn(x, 1, w, tiles_per_seq=tps)

    def seq_last(a):
        return a[tps - 1::tps, HALO_PROMPT - (CONV_W - 1):, :]

    ffn_state = jnp.stack([jnp.concatenate([seq_last(la0), seq_last(lg0)], axis=-1),
                           jnp.concatenate([seq_last(la1), seq_last(lg1)], axis=-1)])
    return (x.reshape(b, s, D_MODEL),
            c_kv.reshape(b, s // PAGE_SIZE, PAGE_SIZE, KV_LORA),
            k_r.reshape(b, s // PAGE_SIZE, PAGE_SIZE, QK_ROPE),
            seq_last(lc), ffn_state)


def _sample_trunk(x_sample, cache_c, cache_r, state_conv_b, state_ffn_conv, page_table, w):
    b, t, _ = x_sample.shape
    n = b * t
    past_len = page_table.shape[1] * cache_c.shape[1]
    tm_rows = lambda a: jnp.swapaxes(a, 0, 1).reshape(a.shape[1] * a.shape[0], a.shape[-1])
    bm_rows = lambda a, k: jnp.swapaxes(a.reshape(k, b, a.shape[-1]), 0, 1)
    x = tm_rows(x_sample)
    pos = past_len + jnp.repeat(jnp.arange(t), b)
    tabs = _rope_tables(pos)
    q, c_kv, k_r = _mla_proj(x, tabs, w, False, n // TM)

    q_lat = _absorb_q(q, w['w_ukt'])
    q_lat = jnp.transpose(q_lat.reshape(N_HEADS, t, b, KV_LORA), (2, 1, 0, 3)).reshape(b, t * N_HEADS, KV_LORA)
    q_rope = q.reshape(t, b, N_HEADS, HEAD_PAD)[..., QK_NOPE:QK_HEAD]
    q_rope = jnp.transpose(q_rope, (1, 0, 2, 3)).reshape(b, t * N_HEADS, QK_ROPE)
    pad_new = lambda a: jnp.pad(bm_rows(a, t), ((0, 0), (0, SUBLANES - t), (0, 0)))
    cache_rt = jnp.swapaxes(cache_r, 1, 2)
    o_lat = _decode_attention(q_lat, q_rope, pad_new(c_kv), pad_new(k_r), cache_c, cache_rt, page_table, t)
    o_lat = jnp.transpose(o_lat.reshape(b, t, N_HEADS, KV_LORA), (2, 1, 0, 3)).reshape(N_HEADS, n, KV_LORA)
    o = _unabsorb_o(o_lat.astype(BF16), w['w_uv_pair'])

    x = _out_proj_ln(x, o, w['w_a_out'], w['ln_g'][0], w['ln_b'][0])
    x, la0, lg0 = _conv_ffn(x, 0, w, state=tm_rows(state_ffn_conv[0]), stride=b)
    x, lc = _short_conv(x, w, state=tm_rows(state_conv_b), stride=b)
    x, la1, lg1 = _conv_ffn(x, 1, w, state=tm_rows(state_ffn_conv[1]), stride=b)

    k = CONV_W - 1
    ffn_state = jnp.stack([bm_rows(jnp.concatenate([la0[0], lg0[0]], axis=-1), k),
                           bm_rows(jnp.concatenate([la1[0], lg1[0]], axis=-1), k)])
    return (bm_rows(x, t), bm_rows(c_kv, t), bm_rows(k_r, t), bm_rows(lc[0], k), ffn_state)


def kernel(x_prompt, x_sample, cache_kv_latent, cache_k_rope, state_conv_b, state_ffn_conv, page_table,
           w_a_down, g_q_norm, g_kv_norm, w_a_uq, w_a_uk, w_a_uv, w_a_out,
           w_b_in, w_b_conv, w_b_out, w_f_up, w_f_conv, b_f_conv, w_f_down, ln_g, ln_b):
    w = _prepare_weights(w_a_down, g_q_norm, g_kv_norm, w_a_uq, w_a_uk, w_a_uv, w_a_out,
                         w_b_in, w_b_conv, w_b_out, w_f_up, w_f_conv, b_f_conv, w_f_down, ln_g, ln_b)
    y_p, lat_p, rope_p, conv_p, ffn_p = _prompt_trunk(x_prompt, w)
    y_s, lat_s, rope_s, conv_s, ffn_s = _sample_trunk(
        x_sample, cache_kv_latent, cache_k_rope, state_conv_b, state_ffn_conv, page_table, w)
    return (y_p, y_s, lat_p, rope_p, lat_s, rope_s, conv_p, conv_s, ffn_p, ffn_s)
```

```python
import functools
import math

import jax
import jax.numpy as jnp
from jax import lax
from jax.experimental import pallas as pl
from jax.experimental.pallas import tpu as pltpu

D_MODEL = 1024
N_HEADS = 16
Q_LORA = 384
KV_LORA = 256
QK_NOPE = 64
QK_ROPE = 32
HALF_ROPE = QK_ROPE // 2
V_HEAD = 64
QK_HEAD = QK_NOPE + QK_ROPE
ROPE_THETA = 10000.0
ATTN_SCALE = QK_HEAD ** -0.5
Q_SCALE = ATTN_SCALE * math.log2(math.e)
CONV_W = 3
D_FF = 2816
RMS_EPS = 1e-6
LN_EPS = 1e-5
DEPTH = 2
DEEPNORM_ALPHA = (2 * DEPTH) ** 0.25
PAGE_SIZE = 128

LANES = 128
SUBLANES = 8
HEAD_PAD = LANES
VMEM_LIMIT = 56 * 1024 * 1024

TM = 512
FF_TM = 1024
TQ = 512
TK = 512
FLASH_HEADS = 8
RQ = 128
QK_AHEAD = 4
FF_CHUNK = 1408
N_FF_CHUNKS = D_FF // FF_CHUNK
FF_ROWS = 128
ROW_BLOCK = 128
MLA_ROWS = 256
UP_AHEAD = 1
PAGES_PER_GROUP = 16
DECODE_SLOTS = 4
HALO_PROMPT = SUBLANES

BF16 = jnp.bfloat16
F32 = jnp.float32


def _dot(a, b):
    return jnp.dot(a, b, preferred_element_type=F32)


def _dot_nt(a, b):
    return lax.dot_general(a, b, (((1,), (1,)), ((), ())), preferred_element_type=F32)


def _layer_norm(v, g, b):
    mu = jnp.mean(v, axis=-1, keepdims=True)
    vc = v - mu
    var = jnp.mean(vc * vc, axis=-1, keepdims=True)
    return vc * lax.rsqrt(var + LN_EPS) * g + b


def _rms_norm(v, g):
    return v * lax.rsqrt(jnp.mean(v * v, axis=-1, keepdims=True) + RMS_EPS) * g


def _software_pipeline(n_blocks, ahead, first_stage, second_stage):
    pending = [first_stage(r) for r in range(min(ahead, n_blocks))]
    for r in range(n_blocks):
        value = pending.pop(0)
        if r + ahead < n_blocks:
            pending.append(first_stage(r + ahead))
        second_stage(r, value)


def _cparams(sem):
    return pltpu.CompilerParams(dimension_semantics=sem, vmem_limit_bytes=VMEM_LIMIT)


def _mla_proj_kernel(emit_kv, x_ref, wd_ref, gq_ref, gkv_ref, wq_ref, cosg_ref, sing_ref,
                     ckr_ref, skr_ref, *rest):
    if emit_kv:
        wk_ref, e_ref, wv_ref, q_ref, ckv_ref, kr_ref, k_ref, v_ref = rest
    else:
        q_ref, ckv_ref, kr_ref = rest
    qw = N_HEADS * HEAD_PAD
    o = Q_LORA + KV_LORA

    def down(r):
        rows = slice(r * MLA_ROWS, (r + 1) * MLA_ROWS)
        return _dot(x_ref[rows, :].astype(BF16), wd_ref[...])

    def rest_of(r, d):
        rows = slice(r * MLA_ROWS, (r + 1) * MLA_ROWS)
        c_q = _rms_norm(d[:, :Q_LORA], gq_ref[...])
        c_kv = _rms_norm(d[:, Q_LORA:o], gkv_ref[...])
        ckv_ref[rows, :] = c_kv
        krg = d[:, o:o + LANES] * ckr_ref[rows, :] + d[:, o + LANES:o + 2 * LANES] * skr_ref[rows, :]
        kr_ref[rows, :] = krg[:, :QK_ROPE]
        q2 = _dot(c_q.astype(BF16), wq_ref[...])
        cosg = cosg_ref[rows, :] * Q_SCALE
        sing = sing_ref[rows, :] * Q_SCALE
        for h in range(N_HEADS):
            lo = h * HEAD_PAD
            q_ref[rows, lo:lo + HEAD_PAD] = (q2[:, lo:lo + HEAD_PAD] * cosg
                                             + q2[:, qw + lo:qw + lo + HEAD_PAD] * sing).astype(BF16)
        if emit_kv:
            ckvb = c_kv.astype(BF16)
            kt = _dot_nt(wk_ref[...], ckvb) + _dot_nt(e_ref[...], krg.astype(BF16))
            k_ref[:, rows] = kt.astype(BF16)
            v_ref[rows, :] = _dot(ckvb, wv_ref[...]).astype(BF16)

    _software_pipeline(x_ref.shape[0] // MLA_ROWS, 1, down, rest_of)


def _mla_proj(x, tabs, w, emit_kv, pos_tiles):
    t = x.shape[0]
    n = t // TM
    qw = N_HEADS * HEAD_PAD
    const = lambda i: (0, 0)
    row = lambda i: (i, 0)
    tab = lambda i: (i % pos_tiles, 0)
    in_specs = [
        pl.BlockSpec((TM, D_MODEL), row),
        pl.BlockSpec(w['wd'].shape, const),
        pl.BlockSpec((1, Q_LORA), const),
        pl.BlockSpec((1, KV_LORA), const),
        pl.BlockSpec(w['wq'].shape, const),
    ] + [pl.BlockSpec((TM, LANES), tab)] * 4
    args = [x, w['wd'], w['gq'], w['gkv'], w['wq']] + list(tabs)
    out_shape = [jax.ShapeDtypeStruct((t, qw), BF16),
                 jax.ShapeDtypeStruct((t, KV_LORA), F32),
                 jax.ShapeDtypeStruct((t, QK_ROPE), F32)]
    out_specs = [pl.BlockSpec((TM, qw), row), pl.BlockSpec((TM, KV_LORA), row),
                 pl.BlockSpec((TM, QK_ROPE), row)]
    if emit_kv:
        in_specs += [pl.BlockSpec(w['wk'].shape, const), pl.BlockSpec(w['e'].shape, const),
                     pl.BlockSpec(w['wv'].shape, const)]
        args += [w['wk'], w['e'], w['wv']]
        out_shape += [jax.ShapeDtypeStruct((qw, t), BF16),
                      jax.ShapeDtypeStruct((t, N_HEADS * V_HEAD), BF16)]
        out_specs += [pl.BlockSpec((qw, TM), lambda i: (0, i)),
                      pl.BlockSpec((TM, N_HEADS * V_HEAD), row)]
    return pl.pallas_call(
        functools.partial(_mla_proj_kernel, emit_kv),
        grid=(n,), in_specs=in_specs, out_specs=out_specs, out_shape=out_shape,
        compiler_params=_cparams(("parallel",)), name="mla_proj_kv" if emit_kv else "mla_proj",
    )(*args)


def _flash_kernel(qi_tab, ki_tab, q_ref, k_ref, v_ref, o_ref, m_sc, l_sc, acc_sc):
    step = pl.program_id(2)
    qi = qi_tab[step]
    ki = ki_tab[step]

    @pl.when(ki == 0)
    def _():
        m_sc[...] = jnp.full(m_sc.shape, -jnp.inf, F32)
        l_sc[...] = jnp.zeros(l_sc.shape, F32)
        acc_sc[...] = jnp.zeros(acc_sc.shape, F32)

    def tile(diagonal):
        chains = [(hh, r) for hh in range(FLASH_HEADS) for r in range(TQ // RQ)]

        def n_keys(r):
            return (r + 1) * RQ if diagonal else TK

        def scores(chain):
            hh, r = chain
            hs = slice(hh * HEAD_PAD, (hh + 1) * HEAD_PAD)
            return _dot(q_ref[r * RQ:(r + 1) * RQ, hs], k_ref[hs, 0:n_keys(r)])

        def finish(chain, s):
            hh, r = chain
            rows = slice(r * RQ, (r + 1) * RQ)
            nk = n_keys(r)
            if diagonal:
                row = lax.broadcasted_iota(jnp.int32, (RQ, nk), 0) + r * RQ
                col = lax.broadcasted_iota(jnp.int32, (RQ, nk), 1)
                s = jnp.where(col <= row, s, -jnp.inf)
            chunks = [s[:, c * LANES:(c + 1) * LANES] for c in range(nk // LANES)]
            mx = chunks[0]
            for c in chunks[1:]:
                mx = jnp.maximum(mx, c)
            m_prev = m_sc[hh, rows, :]
            m_new = jnp.maximum(m_prev, jnp.max(mx, axis=-1, keepdims=True))
            alpha = jnp.exp2(m_prev - m_new)
            ps = [jnp.exp2(c - m_new) for c in chunks]
            psum = ps[0]
            for pc in ps[1:]:
                psum = psum + pc
            p = jnp.concatenate(ps, axis=1).astype(BF16) if len(ps) > 1 else ps[0].astype(BF16)
            l_sc[hh, rows, :] = alpha * l_sc[hh, rows, :] + psum
            vs = slice((hh // 2) * 2 * V_HEAD, (hh // 2 + 1) * 2 * V_HEAD)
            acc_sc[hh, rows, :] = alpha * acc_sc[hh, rows, :] + _dot(p, v_ref[0:nk, vs])
            m_sc[hh, rows, :] = m_new

        _software_pipeline(len(chains), QK_AHEAD, lambda i: scores(chains[i]),
                           lambda i, s: finish(chains[i], s))

    @pl.when(ki < qi)
    def _():
        tile(False)

    @pl.when(ki == qi)
    def _():
        tile(True)
        lane = lax.broadcasted_iota(jnp.int32, (TQ, 2 * V_HEAD), 1)
        for pair in range(FLASH_HEADS // 2):
            o0, o1 = [acc_sc[h] / jnp.sum(l_sc[h], axis=-1, keepdims=True) for h in (2 * pair, 2 * pair + 1)]
            o_ref[:, pair * 2 * V_HEAD:(pair + 1) * 2 * V_HEAD] = jnp.where(lane < V_HEAD, o0, o1).astype(BF16)


def _flash_attention(q, k, v, batch, seq):
    nq = seq // TQ
    pairs = [(a, b) for a in range(nq) for b in range(a + 1)]
    qi_tab = jnp.asarray([p[0] for p in pairs], jnp.int32)
    ki_tab = jnp.asarray([p[1] for p in pairs], jnp.int32)
    grid_spec = pltpu.PrefetchScalarGridSpec(
        num_scalar_prefetch=2,
        grid=(batch, N_HEADS // FLASH_HEADS, len(pairs)),
        in_specs=[
            pl.BlockSpec((TQ, FLASH_HEADS * HEAD_PAD), lambda b, h, s, qt, kt: (b * nq + qt[s], h)),
            pl.BlockSpec((FLASH_HEADS * HEAD_PAD, TK), lambda b, h, s, qt, kt: (h, b * nq + kt[s])),
            pl.BlockSpec((TK, FLASH_HEADS * V_HEAD), lambda b, h, s, qt, kt: (b * nq + kt[s], h)),
        ],
        out_specs=pl.BlockSpec((TQ, FLASH_HEADS * V_HEAD), lambda b, h, s, qt, kt: (b * nq + qt[s], h)),
        scratch_shapes=[pltpu.VMEM((FLASH_HEADS, TQ, LANES), F32), pltpu.VMEM((FLASH_HEADS, TQ, LANES), F32),
                        pltpu.VMEM((FLASH_HEADS, TQ, 2 * V_HEAD), F32)],
    )
    return pl.pallas_call(
        _flash_kernel, grid_spec=grid_spec,
        out_shape=jax.ShapeDtypeStruct((batch * seq, N_HEADS * V_HEAD), BF16),
        compiler_params=_cparams(("parallel", "parallel", "arbitrary")), name="flash_attn",
    )(qi_tab, ki_tab, q, k, v)


def _absorb_kernel(q_ref, w_ref, o_ref):
    o_ref[...] = _dot(q_ref[...], w_ref[...]).astype(BF16)


def _absorb_q(q, w_ukt):
    t = q.shape[0]
    return pl.pallas_call(
        _absorb_kernel, grid=(N_HEADS,),
        in_specs=[pl.BlockSpec((t, HEAD_PAD), lambda h: (0, h)),
                  pl.BlockSpec((None, HEAD_PAD, KV_LORA), lambda h: (h, 0, 0))],
        out_specs=pl.BlockSpec((None, t, KV_LORA), lambda h: (h, 0, 0)),
        out_shape=jax.ShapeDtypeStruct((N_HEADS, t, KV_LORA), BF16),
        compiler_params=_cparams(("parallel",)), name="absorb_q",
    )(q, w_ukt)


def _unabsorb_kernel(o_ref, w_ref, out_ref):
    out_ref[...] = (_dot(o_ref[0], w_ref[0]) + _dot(o_ref[1], w_ref[1])).astype(BF16)


def _unabsorb_o(o_lat, w_uv_pair):
    t = o_lat.shape[1]
    return pl.pallas_call(
        _unabsorb_kernel, grid=(N_HEADS // 2,),
        in_specs=[pl.BlockSpec((2, t, KV_LORA), lambda j: (j, 0, 0)),
                  pl.BlockSpec((2, KV_LORA, 2 * V_HEAD), lambda j: (j, 0, 0))],
        out_specs=pl.BlockSpec((t, 2 * V_HEAD), lambda j: (0, j)),
        out_shape=jax.ShapeDtypeStruct((t, N_HEADS * V_HEAD), BF16),
        compiler_params=_cparams(("parallel",)), name="unabsorb_o",
    )(o_lat, w_uv_pair)


def _decode_kernel(n_new, n_groups, pt_ref, ql_ref, qr_ref, cn_ref, rn_ref, cache_c, cache_r, o_ref,
                   cbuf, rbuf, sem_c, sem_r):
    b = pl.program_id(0)
    rows = ql_ref.shape[0]
    gk = PAGES_PER_GROUP * PAGE_SIZE

    def group_copies(seq, g):
        slot = g % DECODE_SLOTS
        copies = []
        for j in range(PAGES_PER_GROUP):
            page = pt_ref[seq, g * PAGES_PER_GROUP + j]
            keys = pl.ds(j * PAGE_SIZE, PAGE_SIZE)
            copies.append(pltpu.make_async_copy(cache_c.at[page], cbuf.at[slot, keys, :], sem_c.at[slot]))
            copies.append(pltpu.make_async_copy(cache_r.at[page], rbuf.at[slot, :, keys], sem_r.at[slot]))
        return copies

    def start(seq, g):
        for cp in group_copies(seq, g):
            cp.start()

    def wait(seq, g):
        for cp in group_copies(seq, g):
            cp.wait()

    @pl.when(b == 0)
    def _():
        for g in range(DECODE_SLOTS):
            start(0, g)

    ql = ql_ref[...]
    qr = qr_ref[...]

    def scores(g):
        slot = g % DECODE_SLOTS
        c = cbuf[slot].astype(BF16)
        s = _dot_nt(ql, c) + _dot(qr, rbuf[slot].astype(BF16))
        return s, c

    m = jnp.full((rows, LANES), -jnp.inf, F32)
    l = jnp.zeros((rows, LANES), F32)
    acc = jnp.zeros((rows, KV_LORA), F32)
    wait(b, 0)
    nxt = scores(0)
    for g in range(n_groups):
        s, c = nxt
        if g + 1 < n_groups:
            wait(b, g + 1)
            nxt = scores(g + 1)
        chunks = [s[:, k * LANES:(k + 1) * LANES] for k in range(gk // LANES)]
        mx = chunks[0]
        for ch in chunks[1:]:
            mx = jnp.maximum(mx, ch)
        m_new = jnp.maximum(m, jnp.max(mx, axis=-1, keepdims=True))
        alpha = jnp.exp2(m - m_new)
        ps = [jnp.exp2(ch - m_new) for ch in chunks]
        psum = ps[0]
        for pc in ps[1:]:
            psum = psum + pc
        l = alpha * l + psum
        acc = alpha[:, 0:1] * acc + _dot(jnp.concatenate(ps, axis=1).astype(BF16), c)
        m = m_new
        if g + DECODE_SLOTS < n_groups:
            start(b, g + DECODE_SLOTS)
        else:
            @pl.when(b + 1 < pl.num_programs(0))
            def _():
                start(b + 1, g + DECODE_SLOTS - n_groups)

    qlf = ql.astype(F32)
    qrf = qr.astype(F32)
    row = lax.broadcasted_iota(jnp.int32, (rows, 1), 0)
    s_new = []
    for j in range(n_new):
        sj = (jnp.sum(qlf * cn_ref[j:j + 1, :], axis=-1, keepdims=True)
              + jnp.sum(qrf * rn_ref[j:j + 1, :], axis=-1, keepdims=True))
        s_new.append(jnp.where(row >= j * N_HEADS, sj, -jnp.inf))
    m_old = m[:, 0:1]
    m_fin = m_old
    for sj in s_new:
        m_fin = jnp.maximum(m_fin, sj)
    a = jnp.exp2(m_old - m_fin)
    l_fin = a * jnp.sum(l, axis=-1, keepdims=True)
    acc = a * acc
    for j in range(n_new):
        pj = jnp.exp2(s_new[j] - m_fin)
        l_fin = l_fin + pj
        acc = acc + pj * cn_ref[j:j + 1, :]
    o_ref[...] = acc / l_fin


def _decode_attention(q_lat, q_rope, c_new, r_new, cache_c, cache_r, page_table, n_new):
    b, rows, _ = q_lat.shape
    n_pages = page_table.shape[1]
    assert n_pages % PAGES_PER_GROUP == 0
    n_groups = n_pages // PAGES_PER_GROUP
    assert n_groups % DECODE_SLOTS == 0
    gk = PAGES_PER_GROUP * PAGE_SIZE
    seq = lambda i, pt: (i, 0, 0)
    any_spec = pl.BlockSpec(memory_space=pl.ANY)
    grid_spec = pltpu.PrefetchScalarGridSpec(
        num_scalar_prefetch=1, grid=(b,),
        in_specs=[pl.BlockSpec((None, rows, KV_LORA), seq), pl.BlockSpec((None, rows, QK_ROPE), seq),
                  pl.BlockSpec((None, SUBLANES, KV_LORA), seq), pl.BlockSpec((None, SUBLANES, QK_ROPE), seq),
                  any_spec, any_spec],
        out_specs=pl.BlockSpec((None, rows, KV_LORA), seq),
        scratch_shapes=[pltpu.VMEM((DECODE_SLOTS, gk, KV_LORA), F32),
                        pltpu.VMEM((DECODE_SLOTS, QK_ROPE, gk), F32),
                        pltpu.SemaphoreType.DMA((DECODE_SLOTS,)),
                        pltpu.SemaphoreType.DMA((DECODE_SLOTS,))],
    )
    return pl.pallas_call(
        functools.partial(_decode_kernel, n_new, n_groups), grid_spec=grid_spec,
        out_shape=jax.ShapeDtypeStruct((b, rows, KV_LORA), F32),
        compiler_params=_cparams(("arbitrary",)), name="decode_attn",
    )(page_table, q_lat, q_rope, c_new, r_new, cache_c, cache_r)


def _out_proj_kernel(x_ref, o_ref, w_ref, g_ref, b_ref, y_ref):
    def proj(r):
        return _dot(o_ref[r * ROW_BLOCK:(r + 1) * ROW_BLOCK, :], w_ref[...])

    def norm(r, mix):
        rows = slice(r * ROW_BLOCK, (r + 1) * ROW_BLOCK)
        y_ref[rows, :] = _layer_norm(DEEPNORM_ALPHA * x_ref[rows, :] + mix, g_ref[...], b_ref[...])

    _software_pipeline(x_ref.shape[0] // ROW_BLOCK, 1, proj, norm)


def _out_proj_ln(x, o, w, g, b):
    t = x.shape[0]
    const = lambda i: (0, 0)
    row = lambda i: (i, 0)
    return pl.pallas_call(
        _out_proj_kernel, grid=(t // TM,),
        in_specs=[pl.BlockSpec((TM, D_MODEL), row), pl.BlockSpec((TM, N_HEADS * V_HEAD), row),
                  pl.BlockSpec(w.shape, const), pl.BlockSpec((1, D_MODEL), const),
                  pl.BlockSpec((1, D_MODEL), const)],
        out_specs=pl.BlockSpec((TM, D_MODEL), row),
        out_shape=jax.ShapeDtypeStruct((t, D_MODEL), F32),
        compiler_params=_cparams(("parallel",)), name="attn_out_ln",
    )(x, o, w, g, b)


class _CausalConv:
    def __init__(self, history, stride):
        self.history = history
        self.stride = stride

    def __call__(self, u, w_ref):
        n = u.shape[0]
        y = w_ref[CONV_W - 1:CONV_W, :] * u
        if self.stride == 1:
            z = jnp.concatenate([self.history, u], axis=0).reshape(n // SUBLANES + 1, SUBLANES, -1)
            sub = lax.broadcasted_iota(jnp.int32, (n // SUBLANES, SUBLANES, u.shape[1]), 1)
            for j in range(CONV_W - 1):
                back = CONV_W - 1 - j
                rot = pltpu.roll(z, back, axis=1)
                shifted = jnp.where(sub < back, rot[:-1], rot[1:]).reshape(n, -1)
                y = y + w_ref[j:j + 1, :] * shifted
            self.history = u[n - SUBLANES:, :]
        else:
            assert n == self.stride
            for j in range(CONV_W - 1):
                y = y + w_ref[j:j + 1, :] * self.history[j]
            self.history = self.history[1:] + [u]
        return y

    def tail(self):
        return self.history if self.stride == 1 else jnp.concatenate(self.history, axis=0)


def _state_blocks(ref, stride):
    return [ref[j * stride:(j + 1) * stride, :] for j in range(CONV_W - 1)]


def _ffn_kernel(tiles_per_seq, stride, x_ref, wua_ref, wug_ref, wca_ref, wcg_ref,
                ba_ref, bg_ref, wd_ref, g_ref, b_ref, *rest):
    sample = tiles_per_seq is None
    j = pl.program_id(1)
    if sample:
        sa_ref, sg_ref, y_ref, la_ref, lg_ref, acc = rest
        conv_a = _CausalConv(_state_blocks(sa_ref, stride), stride)
        conv_g = _CausalConv(_state_blocks(sg_ref, stride), stride)
    else:
        y_ref, la_ref, lg_ref, acc, car_a, car_g = rest

        @pl.when((pl.program_id(0) % tiles_per_seq) == 0)
        def _():
            car_a[j] = jnp.zeros(car_a.shape[1:], F32)
            car_g[j] = jnp.zeros(car_g.shape[1:], F32)

        conv_a = _CausalConv(car_a[j], stride)
        conv_g = _CausalConv(car_g[j], stride)

    @pl.when(j == 0)
    def _():
        acc[...] = jnp.zeros(acc.shape, F32)

    def up(r):
        xb = x_ref[r * FF_ROWS:(r + 1) * FF_ROWS, :].astype(BF16)
        return _dot(xb, wua_ref[...]), _dot(xb, wug_ref[...])

    def down(r, h):
        ha, hg = h
        rows = slice(r * FF_ROWS, (r + 1) * FF_ROWS)
        ya = conv_a(ha, wca_ref) + ba_ref[...]
        yg = conv_g(hg, wcg_ref) + bg_ref[...]
        act = (yg * jax.nn.sigmoid(yg) * ya).astype(BF16)
        acc[rows, :] = acc[rows, :] + _dot(act, wd_ref[...])

    _software_pipeline(x_ref.shape[0] // FF_ROWS, UP_AHEAD, up, down)

    la_ref[...] = conv_a.tail()
    lg_ref[...] = conv_g.tail()
    if not sample:
        car_a[j] = conv_a.tail()
        car_g[j] = conv_g.tail()

    @pl.when(j == pl.num_programs(1) - 1)
    def _():
        y_ref[...] = _layer_norm(DEEPNORM_ALPHA * x_ref[...] + acc[...], g_ref[...], b_ref[...])


def _conv_ffn(x, layer, w, state=None, seq_len=None, stride=1):
    t = x.shape[0]
    sample = state is not None
    tm = t if sample else FF_TM
    n_tiles = t // tm
    halo_rows = (CONV_W - 1) * stride if sample else HALO_PROMPT
    nc = N_FF_CHUNKS
    in_specs = [
        pl.BlockSpec((tm, D_MODEL), lambda i, j: (i, 0)),
        pl.BlockSpec((None, D_MODEL, FF_CHUNK), lambda i, j: (layer, 0, j)),
        pl.BlockSpec((None, D_MODEL, FF_CHUNK), lambda i, j: (layer, 0, nc + j)),
        pl.BlockSpec((None, CONV_W, FF_CHUNK), lambda i, j: (layer, 0, j)),
        pl.BlockSpec((None, CONV_W, FF_CHUNK), lambda i, j: (layer, 0, nc + j)),
        pl.BlockSpec((None, 1, FF_CHUNK), lambda i, j: (layer, 0, j)),
        pl.BlockSpec((None, 1, FF_CHUNK), lambda i, j: (layer, 0, nc + j)),
        pl.BlockSpec((None, FF_CHUNK, D_MODEL), lambda i, j: (layer, j, 0)),
        pl.BlockSpec((None, 1, D_MODEL), lambda i, j: (2 * layer + 1, 0, 0)),
        pl.BlockSpec((None, 1, D_MODEL), lambda i, j: (2 * layer + 1, 0, 0)),
    ]
    args = [x, w['w_up'], w['w_up'], w['w_fconv'], w['w_fconv'], w['b_fconv'], w['b_fconv'],
            w['w_down'], w['ln_g'], w['ln_b']]
    scratch = [pltpu.VMEM((tm, D_MODEL), F32)]
    if sample:
        in_specs += [pl.BlockSpec((halo_rows, FF_CHUNK), lambda i, j: (0, j)),
                     pl.BlockSpec((halo_rows, FF_CHUNK), lambda i, j: (0, nc + j))]
        args += [state, state]
    else:
        scratch += [pltpu.VMEM((nc, halo_rows, FF_CHUNK), F32), pltpu.VMEM((nc, halo_rows, FF_CHUNK), F32)]
    out_shape = [jax.ShapeDtypeStruct((t, D_MODEL), F32),
                 jax.ShapeDtypeStruct((n_tiles, halo_rows, D_FF), F32),
                 jax.ShapeDtypeStruct((n_tiles, halo_rows, D_FF), F32)]
    out_specs = [pl.BlockSpec((tm, D_MODEL), lambda i, j: (i, 0)),
                 pl.BlockSpec((None, halo_rows, FF_CHUNK), lambda i, j: (i, 0, j)),
                 pl.BlockSpec((None, halo_rows, FF_CHUNK), lambda i, j: (i, 0, j))]
    return pl.pallas_call(
        functools.partial(_ffn_kernel, None if sample else seq_len // tm, stride),
        grid=(n_tiles, nc), in_specs=in_specs, out_specs=out_specs, out_shape=out_shape,
        scratch_shapes=scratch,
        compiler_params=_cparams(("arbitrary", "arbitrary")), name="conv_ffn",
    )(*args)


def _sconv_kernel(tiles_per_seq, stride, x_ref, win_ref, wc_ref, wout_ref, g_ref, b_ref, *rest):
    sample = tiles_per_seq is None
    if sample:
        s_ref, y_ref, last_ref = rest
        conv = _CausalConv(_state_blocks(s_ref, stride), stride)
    else:
        y_ref, last_ref, car = rest

        @pl.when((pl.program_id(0) % tiles_per_seq) == 0)
        def _():
            car[...] = jnp.zeros(car.shape, F32)

        conv = _CausalConv(car[...], stride)

    def proj_in(r):
        return _dot(x_ref[r * ROW_BLOCK:(r + 1) * ROW_BLOCK, :].astype(BF16), win_ref[...])

    def mix_out(r, bch):
        rows = slice(r * ROW_BLOCK, (r + 1) * ROW_BLOCK)
        g_b = bch[:, :D_MODEL]
        u = bch[:, D_MODEL:2 * D_MODEL] * bch[:, 2 * D_MODEL:]
        mix = _dot((g_b * conv(u, wc_ref)).astype(BF16), wout_ref[...])
        y_ref[rows, :] = _layer_norm(DEEPNORM_ALPHA * x_ref[rows, :] + mix, g_ref[...], b_ref[...])

    _software_pipeline(x_ref.shape[0] // ROW_BLOCK, 1, proj_in, mix_out)

    last_ref[...] = conv.tail()
    if not sample:
        car[...] = conv.tail()


def _short_conv(x, w, state=None, tiles_per_seq=None, stride=1):
    t = x.shape[0]
    n_tiles = t // TM
    sample = state is not None
    halo_rows = (CONV_W - 1) * stride if sample else HALO_PROMPT
    const = lambda i: (0, 0)
    row = lambda i: (i, 0)
    in_specs = [pl.BlockSpec((TM, D_MODEL), row), pl.BlockSpec(w['w_b_in'].shape, const),
                pl.BlockSpec((CONV_W, D_MODEL), const), pl.BlockSpec(w['w_b_out'].shape, const),
                pl.BlockSpec((None, 1, D_MODEL), lambda i: (2, 0, 0)),
                pl.BlockSpec((None, 1, D_MODEL), lambda i: (2, 0, 0))]
    args = [x, w['w_b_in'], w['w_b_conv'], w['w_b_out'], w['ln_g'], w['ln_b']]
    scratch = []
    if sample:
        in_specs += [pl.BlockSpec((halo_rows, D_MODEL), const)]
        args += [state]
    else:
        scratch += [pltpu.VMEM((halo_rows, D_MODEL), F32)]
    return pl.pallas_call(
        functools.partial(_sconv_kernel, None if sample else tiles_per_seq, stride),
        grid=(n_tiles,), in_specs=in_specs,
        out_specs=[pl.BlockSpec((TM, D_MODEL), row),
                   pl.BlockSpec((None, halo_rows, D_MODEL), lambda i: (i, 0, 0))],
        out_shape=[jax.ShapeDtypeStruct((t, D_MODEL), F32),
                   jax.ShapeDtypeStruct((n_tiles, halo_rows, D_MODEL), F32)],
        scratch_shapes=scratch,
        compiler_params=_cparams(("arbitrary",)), name="short_conv",
    )(*args)


def _rot_cols(w):
    return jnp.concatenate([-w[..., HALF_ROPE:], w[..., :HALF_ROPE]], axis=-1)


def _prepare_weights(w_a_down, g_q_norm, g_kv_norm, w_a_uq, w_a_uk, w_a_uv, w_a_out,
                     w_b_in, w_b_conv, w_b_out, w_f_up, w_f_conv, b_f_conv, w_f_down, ln_g, ln_b):
    o = Q_LORA + KV_LORA
    w_kr = w_a_down[:, o:]
    zpad = jnp.zeros((D_MODEL, LANES - QK_ROPE), F32)
    wd = jnp.concatenate([w_a_down[:, :o], w_kr, zpad, _rot_cols(w_kr), zpad], axis=1)

    wq3 = w_a_uq.reshape(Q_LORA, N_HEADS, QK_HEAD)
    zq = jnp.zeros((Q_LORA, N_HEADS, HEAD_PAD - QK_HEAD), F32)
    q_main = jnp.concatenate([wq3, zq], axis=-1)
    q_rot = jnp.concatenate([jnp.zeros((Q_LORA, N_HEADS, QK_NOPE), F32), _rot_cols(wq3[..., QK_NOPE:]), zq],
                            axis=-1)
    wq = jnp.concatenate([q_main.reshape(Q_LORA, -1), q_rot.reshape(Q_LORA, -1)], axis=1)

    wk = jnp.concatenate([w_a_uk, jnp.zeros((KV_LORA, N_HEADS, HEAD_PAD - QK_NOPE), F32)], axis=-1)
    wk = wk.reshape(KV_LORA, N_HEADS * HEAD_PAD)
    src = jnp.arange(LANES)[:, None]
    dst = jnp.arange(N_HEADS * HEAD_PAD)[None, :]
    e = ((src < QK_ROPE) & ((dst % HEAD_PAD) == QK_NOPE + src)).astype(BF16)

    w_ukt = jnp.transpose(w_a_uk, (1, 2, 0))
    w_ukt = jnp.concatenate([w_ukt, jnp.zeros((N_HEADS, HEAD_PAD - QK_NOPE, KV_LORA), F32)], axis=1)
    w_uvh = jnp.transpose(w_a_uv, (1, 0, 2))
    zv = jnp.zeros_like(w_uvh)
    even = (jnp.arange(N_HEADS) % 2 == 0)[:, None, None]
    w_uv_pair = jnp.where(even, jnp.concatenate([w_uvh, zv], -1), jnp.concatenate([zv, w_uvh], -1))

    return dict(
        wd=wd.astype(BF16), gq=g_q_norm.reshape(1, -1), gkv=g_kv_norm.reshape(1, -1), wq=wq.astype(BF16),
        wk=wk.T.astype(BF16), e=e.T, wv=w_a_uv.reshape(KV_LORA, -1).astype(BF16),
        w_ukt=w_ukt.astype(BF16), w_uv_pair=w_uv_pair.astype(BF16), w_a_out=w_a_out.astype(BF16),
        w_b_in=w_b_in.astype(BF16), w_b_conv=w_b_conv, w_b_out=w_b_out.astype(BF16),
        w_up=w_f_up.astype(BF16), w_fconv=w_f_conv, b_fconv=b_f_conv.reshape(DEPTH, 1, 2 * D_FF),
        w_down=w_f_down.astype(BF16), ln_g=ln_g.reshape(2 * DEPTH, 1, D_MODEL),
        ln_b=ln_b.reshape(2 * DEPTH, 1, D_MODEL),
    )


def _rope_tables(pos):
    inv = 1.0 / (ROPE_THETA ** (jnp.arange(0, QK_ROPE, 2, dtype=F32) / QK_ROPE))
    ang = pos.astype(F32)[:, None] * inv[None, :]
    cos, sin = jnp.cos(ang), jnp.sin(ang)
    p = pos.shape[0]
    one = jnp.ones((p, QK_NOPE), F32)
    z64 = jnp.zeros((p, QK_NOPE), F32)
    z32 = jnp.zeros((p, HEAD_PAD - QK_HEAD), F32)
    z96 = jnp.zeros((p, LANES - QK_ROPE), F32)
    cosg = jnp.concatenate([one, cos, cos, z32], axis=1)
    sing = jnp.concatenate([z64, sin, sin, z32], axis=1)
    ckr = jnp.concatenate([cos, cos, z96], axis=1)
    skr = jnp.concatenate([sin, sin, z96], axis=1)
    return cosg, sing, ckr, skr


def _prompt_trunk(x_prompt, w):
    b, s, _ = x_prompt.shape
    x = x_prompt.reshape(b * s, D_MODEL)
    tps = s // TM
    tabs = _rope_tables(jnp.arange(s))
    q, c_kv, k_r, k, v = _mla_proj(x, tabs, w, True, tps)
    o = _flash_attention(q, k, v, b, s)
    x = _out_proj_ln(x, o, w['w_a_out'], w['ln_g'][0], w['ln_b'][0])
    x, la0, lg0 = _conv_ffn(x, 0, w, seq_len=s)
    x, lc = _short_conv(x, w, tiles_per_seq=tps)
    x, la1, lg1 = _conv_ffn(x, 1, w, seq_len=s)

    def seq_last(a, tiles=tps):
        return a[tiles - 1::tiles, HALO_PROMPT - (CONV_W - 1):, :]

    ffn_last = lambda a: seq_last(a, s // FF_TM)
    ffn_state = jnp.stack([jnp.concatenate([ffn_last(la0), ffn_last(lg0)], axis=-1),
                           jnp.concatenate([ffn_last(la1), ffn_last(lg1)], axis=-1)])
    return (x.reshape(b, s, D_MODEL),
            c_kv.reshape(b, s // PAGE_SIZE, PAGE_SIZE, KV_LORA),
            k_r.reshape(b, s // PAGE_SIZE, PAGE_SIZE, QK_ROPE),
            seq_last(lc), ffn_state)


def _sample_trunk(x_sample, cache_c, cache_r, state_conv_b, state_ffn_conv, page_table, w):
    b, t, _ = x_sample.shape
    n = b * t
    past_len = page_table.shape[1] * cache_c.shape[1]
    tm_rows = lambda a: jnp.swapaxes(a, 0, 1).reshape(a.shape[1] * a.shape[0], a.shape[-1])
    bm_rows = lambda a, k: jnp.swapaxes(a.reshape(k, b, a.shape[-1]), 0, 1)
    x = tm_rows(x_sample)
    pos = past_len + jnp.repeat(jnp.arange(t), b)
    tabs = _rope_tables(pos)
    q, c_kv, k_r = _mla_proj(x, tabs, w, False, n // TM)

    q_lat = _absorb_q(q, w['w_ukt'])
    q_lat = jnp.transpose(q_lat.reshape(N_HEADS, t, b, KV_LORA), (2, 1, 0, 3)).reshape(b, t * N_HEADS, KV_LORA)
    q_rope = q.reshape(t, b, N_HEADS, HEAD_PAD)[..., QK_NOPE:QK_HEAD]
    q_rope = jnp.transpose(q_rope, (1, 0, 2, 3)).reshape(b, t * N_HEADS, QK_ROPE)
    pad_new = lambda a: jnp.pad(bm_rows(a, t), ((0, 0), (0, SUBLANES - t), (0, 0)))
    cache_rt = jnp.swapaxes(cache_r, 1, 2)
    o_lat = _decode_attention(q_lat, q_rope, pad_new(c_kv), pad_new(k_r), cache_c, cache_rt, page_table, t)
    o_lat = jnp.transpose(o_lat.reshape(b, t, N_HEADS, KV_LORA), (2, 1, 0, 3)).reshape(N_HEADS, n, KV_LORA)
    o = _unabsorb_o(o_lat.astype(BF16), w['w_uv_pair'])

    x = _out_proj_ln(x, o, w['w_a_out'], w['ln_g'][0], w['ln_b'][0])
    x, la0, lg0 = _conv_ffn(x, 0, w, state=tm_rows(state_ffn_conv[0]), stride=b)
    x, lc = _short_conv(x, w, state=tm_rows(state_conv_b), stride=b)
    x, la1, lg1 = _conv_ffn(x, 1, w, state=tm_rows(state_ffn_conv[1]), stride=b)

    k = CONV_W - 1
    ffn_state = jnp.stack([bm_rows(jnp.concatenate([la0[0], lg0[0]], axis=-1), k),
                           bm_rows(jnp.concatenate([la1[0], lg1[0]], axis=-1), k)])
    return (bm_rows(x, t), bm_rows(c_kv, t), bm_rows(k_r, t), bm_rows(lc[0], k), ffn_state)


def kernel(x_prompt, x_sample, cache_kv_latent, cache_k_rope, state_conv_b, state_ffn_conv, page_table,
           w_a_down, g_q_norm, g_kv_norm, w_a_uq, w_a_uk, w_a_uv, w_a_out,
           w_b_in, w_b_conv, w_b_out, w_f_up, w_f_conv, b_f_conv, w_f_down, ln_g, ln_b):
    w = _prepare_weights(w_a_down, g_q_norm, g_kv_norm, w_a_uq, w_a_uk, w_a_uv, w_a_out,
                         w_b_in, w_b_conv, w_b_out, w_f_up, w_f_conv, b_f_conv, w_f_down, ln_g, ln_b)
    y_p, lat_p, rope_p, conv_p, ffn_p = _prompt_trunk(x_prompt, w)
    y_s, lat_s, rope_s, conv_s, ffn_s = _sample_trunk(
        x_sample, cache_kv_latent, cache_k_rope, state_conv_b, state_ffn_conv, page_table, w)
    return (y_p, y_s, lat_p, rope_p, lat_s, rope_s, conv_p, conv_s, ffn_p, ffn_s)
```

```python
import functools
import math

import jax
import jax.numpy as jnp
from jax import lax
from jax.experimental import pallas as pl
from jax.experimental.pallas import tpu as pltpu

D_MODEL = 1024
N_HEADS = 16
Q_LORA = 384
KV_LORA = 256
QK_NOPE = 64
QK_ROPE = 32
HALF_ROPE = QK_ROPE // 2
V_HEAD = 64
QK_HEAD = QK_NOPE + QK_ROPE
ROPE_THETA = 10000.0
ATTN_SCALE = QK_HEAD ** -0.5
Q_SCALE = ATTN_SCALE * math.log2(math.e)
CONV_W = 3
D_FF = 2816
RMS_EPS = 1e-6
LN_EPS = 1e-5
DEPTH = 2
DEEPNORM_ALPHA = (2 * DEPTH) ** 0.25
PAGE_SIZE = 128

LANES = 128
SUBLANES = 8
HEAD_PAD = LANES
VMEM_LIMIT = 56 * 1024 * 1024

TM = 512
FF_TM = 1024
TQ = 512
TK = 512
FLASH_HEADS = 8
RQ = 128
QK_AHEAD = 4
FF_CHUNK = 1408
N_FF_CHUNKS = D_FF // FF_CHUNK
FF_ROWS = 128
ROW_BLOCK = 128
MLA_ROWS = 256
UP_AHEAD = 2
PAGES_PER_GROUP = 16
DECODE_SLOTS = 4
HALO_PROMPT = SUBLANES

BF16 = jnp.bfloat16
F32 = jnp.float32


def _dot(a, b):
    return jnp.dot(a, b, preferred_element_type=F32)


def _dot_nt(a, b):
    return lax.dot_general(a, b, (((1,), (1,)), ((), ())), preferred_element_type=F32)


def _layer_norm(v, g, b):
    mu = jnp.mean(v, axis=-1, keepdims=True)
    vc = v - mu
    var = jnp.mean(vc * vc, axis=-1, keepdims=True)
    return vc * lax.rsqrt(var + LN_EPS) * g + b


def _rms_norm(v, g):
    return v * lax.rsqrt(jnp.mean(v * v, axis=-1, keepdims=True) + RMS_EPS) * g


def _software_pipeline(n_blocks, ahead, first_stage, second_stage):
    pending = [first_stage(r) for r in range(min(ahead, n_blocks))]
    for r in range(n_blocks):
        value = pending.pop(0)
        if r + ahead < n_blocks:
            pending.append(first_stage(r + ahead))
        second_stage(r, value)


def _cparams(sem):
    return pltpu.CompilerParams(dimension_semantics=sem, vmem_limit_bytes=VMEM_LIMIT)


def _mla_proj_kernel(emit_kv, x_ref, wd_ref, gq_ref, gkv_ref, wq_ref, cosg_ref, sing_ref,
                     ckr_ref, skr_ref, *rest):
    if emit_kv:
        wk_ref, e_ref, wv_ref, q_ref, ckv_ref, kr_ref, k_ref, v_ref = rest
    else:
        q_ref, ckv_ref, kr_ref = rest
    qw = N_HEADS * HEAD_PAD
    o = Q_LORA + KV_LORA

    def down(r):
        rows = slice(r * MLA_ROWS, (r + 1) * MLA_ROWS)
        return _dot(x_ref[rows, :].astype(BF16), wd_ref[...])

    def rest_of(r, d):
        rows = slice(r * MLA_ROWS, (r + 1) * MLA_ROWS)
        c_q = _rms_norm(d[:, :Q_LORA], gq_ref[...])
        c_kv = _rms_norm(d[:, Q_LORA:o], gkv_ref[...])
        ckv_ref[rows, :] = c_kv
        krg = d[:, o:o + LANES] * ckr_ref[rows, :] + d[:, o + LANES:o + 2 * LANES] * skr_ref[rows, :]
        kr_ref[rows, :] = krg[:, :QK_ROPE]
        q2 = _dot(c_q.astype(BF16), wq_ref[...])
        cosg = cosg_ref[rows, :] * Q_SCALE
        sing = sing_ref[rows, :] * Q_SCALE
        for h in range(N_HEADS):
            lo = h * HEAD_PAD
            q_ref[rows, lo:lo + HEAD_PAD] = (q2[:, lo:lo + HEAD_PAD] * cosg
                                             + q2[:, qw + lo:qw + lo + HEAD_PAD] * sing).astype(BF16)
        if emit_kv:
            ckvb = c_kv.astype(BF16)
            kt = _dot_nt(wk_ref[...], ckvb) + _dot_nt(e_ref[...], krg.astype(BF16))
            k_ref[:, rows] = kt.astype(BF16)
            v_ref[rows, :] = _dot(ckvb, wv_ref[...]).astype(BF16)

    _software_pipeline(x_ref.shape[0] // MLA_ROWS, 1, down, rest_of)


def _mla_proj(x, tabs, w, emit_kv, pos_tiles):
    t = x.shape[0]
    n = t // TM
    qw = N_HEADS * HEAD_PAD
    const = lambda i: (0, 0)
    row = lambda i: (i, 0)
    tab = lambda i: (i % pos_tiles, 0)
    in_specs = [
        pl.BlockSpec((TM, D_MODEL), row),
        pl.BlockSpec(w['wd'].shape, const),
        pl.BlockSpec((1, Q_LORA), const),
        pl.BlockSpec((1, KV_LORA), const),
        pl.BlockSpec(w['wq'].shape, const),
    ] + [pl.BlockSpec((TM, LANES), tab)] * 4
    args = [x, w['wd'], w['gq'], w['gkv'], w['wq']] + list(tabs)
    out_shape = [jax.ShapeDtypeStruct((t, qw), BF16),
                 jax.ShapeDtypeStruct((t, KV_LORA), F32),
                 jax.ShapeDtypeStruct((t, QK_ROPE), F32)]
    out_specs = [pl.BlockSpec((TM, qw), row), pl.BlockSpec((TM, KV_LORA), row),
                 pl.BlockSpec((TM, QK_ROPE), row)]
    if emit_kv:
        in_specs += [pl.BlockSpec(w['wk'].shape, const), pl.BlockSpec(w['e'].shape, const),
                     pl.BlockSpec(w['wv'].shape, const)]
        args += [w['wk'], w['e'], w['wv']]
        out_shape += [jax.ShapeDtypeStruct((qw, t), BF16),
                      jax.ShapeDtypeStruct((t, N_HEADS * V_HEAD), BF16)]
        out_specs += [pl.BlockSpec((qw, TM), lambda i: (0, i)),
                      pl.BlockSpec((TM, N_HEADS * V_HEAD), row)]
    return pl.pallas_call(
        functools.partial(_mla_proj_kernel, emit_kv),
        grid=(n,), in_specs=in_specs, out_specs=out_specs, out_shape=out_shape,
        compiler_params=_cparams(("parallel",)), name="mla_proj_kv" if emit_kv else "mla_proj",
    )(*args)


def _flash_kernel(qi_tab, ki_tab, q_ref, k_ref, v_ref, o_ref, m_sc, l_sc, acc_sc):
    step = pl.program_id(2)
    qi = qi_tab[step]
    ki = ki_tab[step]

    @pl.when(ki == 0)
    def _():
        m_sc[...] = jnp.full(m_sc.shape, -jnp.inf, F32)
        l_sc[...] = jnp.zeros(l_sc.shape, F32)
        acc_sc[...] = jnp.zeros(acc_sc.shape, F32)

    def tile(diagonal):
        chains = [(hh, r) for hh in range(FLASH_HEADS) for r in range(TQ // RQ)]

        def n_keys(r):
            return (r + 1) * RQ if diagonal else TK

        def scores(chain):
            hh, r = chain
            hs = slice(hh * HEAD_PAD, (hh + 1) * HEAD_PAD)
            return _dot(q_ref[r * RQ:(r + 1) * RQ, hs], k_ref[hs, 0:n_keys(r)])

        def finish(chain, s):
            hh, r = chain
            rows = slice(r * RQ, (r + 1) * RQ)
            nk = n_keys(r)
            if diagonal:
                row = lax.broadcasted_iota(jnp.int32, (RQ, nk), 0) + r * RQ
                col = lax.broadcasted_iota(jnp.int32, (RQ, nk), 1)
                s = jnp.where(col <= row, s, -jnp.inf)
            chunks = [s[:, c * LANES:(c + 1) * LANES] for c in range(nk // LANES)]
            mx = chunks[0]
            for c in chunks[1:]:
                mx = jnp.maximum(mx, c)
            m_prev = m_sc[hh, rows, :]
            m_new = jnp.maximum(m_prev, jnp.max(mx, axis=-1, keepdims=True))
            alpha = jnp.exp2(m_prev - m_new)
            ps = [jnp.exp2(c - m_new) for c in chunks]
            psum = ps[0]
            for pc in ps[1:]:
                psum = psum + pc
            p = jnp.concatenate(ps, axis=1).astype(BF16) if len(ps) > 1 else ps[0].astype(BF16)
            l_sc[hh, rows, :] = alpha * l_sc[hh, rows, :] + psum
            vs = slice((hh // 2) * 2 * V_HEAD, (hh // 2 + 1) * 2 * V_HEAD)
            acc_sc[hh, rows, :] = alpha * acc_sc[hh, rows, :] + _dot(p, v_ref[0:nk, vs])
            m_sc[hh, rows, :] = m_new

        _software_pipeline(len(chains), QK_AHEAD, lambda i: scores(chains[i]),
                           lambda i, s: finish(chains[i], s))

    @pl.when(ki < qi)
    def _():
        tile(False)

    @pl.when(ki == qi)
    def _():
        tile(True)
        lane = lax.broadcasted_iota(jnp.int32, (TQ, 2 * V_HEAD), 1)
        for pair in range(FLASH_HEADS // 2):
            o0, o1 = [acc_sc[h] / jnp.sum(l_sc[h], axis=-1, keepdims=True) for h in (2 * pair, 2 * pair + 1)]
            o_ref[:, pair * 2 * V_HEAD:(pair + 1) * 2 * V_HEAD] = jnp.where(lane < V_HEAD, o0, o1).astype(BF16)


def _flash_attention(q, k, v, batch, seq):
    nq = seq // TQ
    pairs = [(a, b) for a in range(nq) for b in range(a + 1)]
    qi_tab = jnp.asarray([p[0] for p in pairs], jnp.int32)
    ki_tab = jnp.asarray([p[1] for p in pairs], jnp.int32)
    grid_spec = pltpu.PrefetchScalarGridSpec(
        num_scalar_prefetch=2,
        grid=(batch, N_HEADS // FLASH_HEADS, len(pairs)),
        in_specs=[
            pl.BlockSpec((TQ, FLASH_HEADS * HEAD_PAD), lambda b, h, s, qt, kt: (b * nq + qt[s], h)),
            pl.BlockSpec((FLASH_HEADS * HEAD_PAD, TK), lambda b, h, s, qt, kt: (h, b * nq + kt[s])),
            pl.BlockSpec((TK, FLASH_HEADS * V_HEAD), lambda b, h, s, qt, kt: (b * nq + kt[s], h)),
        ],
        out_specs=pl.BlockSpec((TQ, FLASH_HEADS * V_HEAD), lambda b, h, s, qt, kt: (b * nq + qt[s], h)),
        scratch_shapes=[pltpu.VMEM((FLASH_HEADS, TQ, LANES), F32), pltpu.VMEM((FLASH_HEADS, TQ, LANES), F32),
                        pltpu.VMEM((FLASH_HEADS, TQ, 2 * V_HEAD), F32)],
    )
    return pl.pallas_call(
        _flash_kernel, grid_spec=grid_spec,
        out_shape=jax.ShapeDtypeStruct((batch * seq, N_HEADS * V_HEAD), BF16),
        compiler_params=_cparams(("parallel", "parallel", "arbitrary")), name="flash_attn",
    )(qi_tab, ki_tab, q, k, v)


def _absorb_kernel(q_ref, w_ref, o_ref):
    o_ref[...] = _dot(q_ref[...], w_ref[...]).astype(BF16)


def _absorb_q(q, w_ukt):
    t = q.shape[0]
    return pl.pallas_call(
        _absorb_kernel, grid=(N_HEADS,),
        in_specs=[pl.BlockSpec((t, HEAD_PAD), lambda h: (0, h)),
                  pl.BlockSpec((None, HEAD_PAD, KV_LORA), lambda h: (h, 0, 0))],
        out_specs=pl.BlockSpec((None, t, KV_LORA), lambda h: (h, 0, 0)),
        out_shape=jax.ShapeDtypeStruct((N_HEADS, t, KV_LORA), BF16),
        compiler_params=_cparams(("parallel",)), name="absorb_q",
    )(q, w_ukt)


def _unabsorb_kernel(o_ref, w_ref, out_ref):
    out_ref[...] = (_dot(o_ref[0], w_ref[0]) + _dot(o_ref[1], w_ref[1])).astype(BF16)


def _unabsorb_o(o_lat, w_uv_pair):
    t = o_lat.shape[1]
    return pl.pallas_call(
        _unabsorb_kernel, grid=(N_HEADS // 2,),
        in_specs=[pl.BlockSpec((2, t, KV_LORA), lambda j: (j, 0, 0)),
                  pl.BlockSpec((2, KV_LORA, 2 * V_HEAD), lambda j: (j, 0, 0))],
        out_specs=pl.BlockSpec((t, 2 * V_HEAD), lambda j: (0, j)),
        out_shape=jax.ShapeDtypeStruct((t, N_HEADS * V_HEAD), BF16),
        compiler_params=_cparams(("parallel",)), name="unabsorb_o",
    )(o_lat, w_uv_pair)


def _decode_kernel(n_new, n_groups, pt_ref, ql_ref, qr_ref, cn_ref, rn_ref, cache_c, cache_r, o_ref,
                   cbuf, rbuf, ctbuf, sem_c, sem_r):
    b = pl.program_id(0)
    rows = ql_ref.shape[0]
    gk = PAGES_PER_GROUP * PAGE_SIZE

    def group_copies(seq, g):
        slot = g % DECODE_SLOTS
        copies = []
        for j in range(PAGES_PER_GROUP):
            page = pt_ref[seq, g * PAGES_PER_GROUP + j]
            keys = pl.ds(j * PAGE_SIZE, PAGE_SIZE)
            copies.append(pltpu.make_async_copy(cache_c.at[page], cbuf.at[slot, keys, :], sem_c.at[slot]))
            copies.append(pltpu.make_async_copy(cache_r.at[page], rbuf.at[slot, :, keys], sem_r.at[slot]))
        return copies

    def start(seq, g):
        for cp in group_copies(seq, g):
            cp.start()

    def wait(seq, g):
        for cp in group_copies(seq, g):
            cp.wait()

    @pl.when(b == 0)
    def _():
        for g in range(DECODE_SLOTS):
            start(0, g)

    ql = ql_ref[...]
    qr = qr_ref[...]

    def scores(g):
        slot = g % DECODE_SLOTS
        c = cbuf[slot].astype(BF16)
        ctbuf[...] = c.T
        s = _dot(ql, ctbuf[...]) + _dot(qr, rbuf[slot].astype(BF16))
        return s, c

    m = jnp.full((rows, LANES), -jnp.inf, F32)
    l = jnp.zeros((rows, LANES), F32)
    acc = jnp.zeros((rows, KV_LORA), F32)
    wait(b, 0)
    nxt = scores(0)
    for g in range(n_groups):
        s, c = nxt
        if g + 1 < n_groups:
            wait(b, g + 1)
            nxt = scores(g + 1)
        chunks = [s[:, k * LANES:(k + 1) * LANES] for k in range(gk // LANES)]
        mx = chunks[0]
        for ch in chunks[1:]:
            mx = jnp.maximum(mx, ch)
        m_new = jnp.maximum(m, jnp.max(mx, axis=-1, keepdims=True))
        alpha = jnp.exp2(m - m_new)
        ps = [jnp.exp2(ch - m_new) for ch in chunks]
        psum = ps[0]
        for pc in ps[1:]:
            psum = psum + pc
        l = alpha * l + psum
        acc = alpha[:, 0:1] * acc + _dot(jnp.concatenate(ps, axis=1).astype(BF16), c)
        m = m_new
        if g + DECODE_SLOTS < n_groups:
            start(b, g + DECODE_SLOTS)
        else:
            @pl.when(b + 1 < pl.num_programs(0))
            def _():
                start(b + 1, g + DECODE_SLOTS - n_groups)

    qlf = ql.astype(F32)
    qrf = qr.astype(F32)
    row = lax.broadcasted_iota(jnp.int32, (rows, 1), 0)
    s_new = []
    for j in range(n_new):
        sj = (jnp.sum(qlf * cn_ref[j:j + 1, :], axis=-1, keepdims=True)
              + jnp.sum(qrf * rn_ref[j:j + 1, :], axis=-1, keepdims=True))
        s_new.append(jnp.where(row >= j * N_HEADS, sj, -jnp.inf))
    m_old = m[:, 0:1]
    m_fin = m_old
    for sj in s_new:
        m_fin = jnp.maximum(m_fin, sj)
    a = jnp.exp2(m_old - m_fin)
    l_fin = a * jnp.sum(l, axis=-1, keepdims=True)
    acc = a * acc
    for j in range(n_new):
        pj = jnp.exp2(s_new[j] - m_fin)
        l_fin = l_fin + pj
        acc = acc + pj * cn_ref[j:j + 1, :]
    o_ref[...] = acc / l_fin


def _decode_attention(q_lat, q_rope, c_new, r_new, cache_c, cache_r, page_table, n_new):
    b, rows, _ = q_lat.shape
    n_pages = page_table.shape[1]
    assert n_pages % PAGES_PER_GROUP == 0
    n_groups = n_pages // PAGES_PER_GROUP
    assert n_groups % DECODE_SLOTS == 0
    gk = PAGES_PER_GROUP * PAGE_SIZE
    seq = lambda i, pt: (i, 0, 0)
    any_spec = pl.BlockSpec(memory_space=pl.ANY)
    grid_spec = pltpu.PrefetchScalarGridSpec(
        num_scalar_prefetch=1, grid=(b,),
        in_specs=[pl.BlockSpec((None, rows, KV_LORA), seq), pl.BlockSpec((None, rows, QK_ROPE), seq),
                  pl.BlockSpec((None, SUBLANES, KV_LORA), seq), pl.BlockSpec((None, SUBLANES, QK_ROPE), seq),
                  any_spec, any_spec],
        out_specs=pl.BlockSpec((None, rows, KV_LORA), seq),
        scratch_shapes=[pltpu.VMEM((DECODE_SLOTS, gk, KV_LORA), F32),
                        pltpu.VMEM((DECODE_SLOTS, QK_ROPE, gk), F32),
                        pltpu.VMEM((KV_LORA, gk), BF16),
                        pltpu.SemaphoreType.DMA((DECODE_SLOTS,)),
                        pltpu.SemaphoreType.DMA((DECODE_SLOTS,))],
    )
    return pl.pallas_call(
        functools.partial(_decode_kernel, n_new, n_groups), grid_spec=grid_spec,
        out_shape=jax.ShapeDtypeStruct((b, rows, KV_LORA), F32),
        compiler_params=_cparams(("arbitrary",)), name="decode_attn",
    )(page_table, q_lat, q_rope, c_new, r_new, cache_c, cache_r)


def _out_proj_kernel(x_ref, o_ref, w_ref, g_ref, b_ref, y_ref):
    def proj(r):
        return _dot(o_ref[r * ROW_BLOCK:(r + 1) * ROW_BLOCK, :], w_ref[...])

    def norm(r, mix):
        rows = slice(r * ROW_BLOCK, (r + 1) * ROW_BLOCK)
        y_ref[rows, :] = _layer_norm(DEEPNORM_ALPHA * x_ref[rows, :] + mix, g_ref[...], b_ref[...])

    _software_pipeline(x_ref.shape[0] // ROW_BLOCK, 1, proj, norm)


def _out_proj_ln(x, o, w, g, b):
    t = x.shape[0]
    const = lambda i: (0, 0)
    row = lambda i: (i, 0)
    return pl.pallas_call(
        _out_proj_kernel, grid=(t // TM,),
        in_specs=[pl.BlockSpec((TM, D_MODEL), row), pl.BlockSpec((TM, N_HEADS * V_HEAD), row),
                  pl.BlockSpec(w.shape, const), pl.BlockSpec((1, D_MODEL), const),
                  pl.BlockSpec((1, D_MODEL), const)],
        out_specs=pl.BlockSpec((TM, D_MODEL), row),
        out_shape=jax.ShapeDtypeStruct((t, D_MODEL), F32),
        compiler_params=_cparams(("parallel",)), name="attn_out_ln",
    )(x, o, w, g, b)


class _CausalConv:
    def __init__(self, history, stride):
        self.history = history
        self.stride = stride

    def __call__(self, u, w_ref):
        n = u.shape[0]
        y = w_ref[CONV_W - 1:CONV_W, :] * u
        if self.stride == 1:
            z = jnp.concatenate([self.history, u], axis=0).reshape(n // SUBLANES + 1, SUBLANES, -1)
            sub = lax.broadcasted_iota(jnp.int32, (n // SUBLANES, SUBLANES, u.shape[1]), 1)
            for j in range(CONV_W - 1):
                back = CONV_W - 1 - j
                rot = pltpu.roll(z, back, axis=1)
                shifted = jnp.where(sub < back, rot[:-1], rot[1:]).reshape(n, -1)
                y = y + w_ref[j:j + 1, :] * shifted
            self.history = u[n - SUBLANES:, :]
        else:
            assert n == self.stride
            for j in range(CONV_W - 1):
                y = y + w_ref[j:j + 1, :] * self.history[j]
            self.history = self.history[1:] + [u]
        return y

    def tail(self):
        return self.history if self.stride == 1 else jnp.concatenate(self.history, axis=0)


def _state_blocks(ref, stride):
    return [ref[j * stride:(j + 1) * stride, :] for j in range(CONV_W - 1)]


def _ffn_kernel(tiles_per_seq, stride, kc, x_ref, wua_ref, wug_ref, wca_ref, wcg_ref,
                ba_ref, bg_ref, wd_ref, g_ref, b_ref, *rest):
    sample = tiles_per_seq is None
    fused_ln = kc == N_FF_CHUNKS
    j = pl.program_id(1)
    cols = [slice(c * FF_CHUNK, (c + 1) * FF_CHUNK) for c in range(kc)]
    if sample:
        sa_ref, sg_ref, y_ref, la_ref, lg_ref = rest[:5]
        conv_a = [_CausalConv(_state_blocks(sa_ref.at[:, cs], stride), stride) for cs in cols]
        conv_g = [_CausalConv(_state_blocks(sg_ref.at[:, cs], stride), stride) for cs in cols]
    else:
        y_ref, la_ref, lg_ref = rest[:3]
        car_a, car_g = rest[-2:]

        @pl.when((pl.program_id(0) % tiles_per_seq) == 0)
        def _():
            car_a[j] = jnp.zeros(car_a.shape[1:], F32)
            car_g[j] = jnp.zeros(car_g.shape[1:], F32)

        conv_a = [_CausalConv(car_a[j, :, cs], stride) for cs in cols]
        conv_g = [_CausalConv(car_g[j, :, cs], stride) for cs in cols]

    if not fused_ln:
        acc = rest[5] if sample else rest[3]

        @pl.when(j == 0)
        def _():
            acc[...] = jnp.zeros(acc.shape, F32)

    units = [(r, c) for r in range(x_ref.shape[0] // FF_ROWS) for c in range(kc)]
    partial = {}

    def up(i):
        r, c = units[i]
        xb = x_ref[r * FF_ROWS:(r + 1) * FF_ROWS, :].astype(BF16)
        return _dot(xb, wua_ref[:, cols[c]]), _dot(xb, wug_ref[:, cols[c]])

    def down(i, h):
        r, c = units[i]
        ha, hg = h
        rows = slice(r * FF_ROWS, (r + 1) * FF_ROWS)
        ya = conv_a[c](ha, wca_ref.at[:, cols[c]]) + ba_ref[:, cols[c]]
        yg = conv_g[c](hg, wcg_ref.at[:, cols[c]]) + bg_ref[:, cols[c]]
        act = (yg * jax.nn.sigmoid(yg) * ya).astype(BF16)
        total = _dot(act, wd_ref[cols[c], :])
        if c > 0:
            total = partial.pop(r) + total
        if c < kc - 1:
            partial[r] = total
        elif fused_ln:
            y_ref[rows, :] = _layer_norm(DEEPNORM_ALPHA * x_ref[rows, :] + total, g_ref[...], b_ref[...])
        else:
            acc[rows, :] = acc[rows, :] + total

    _software_pipeline(len(units), UP_AHEAD, up, down)

    for c in range(kc):
        la_ref[:, cols[c]] = conv_a[c].tail()
        lg_ref[:, cols[c]] = conv_g[c].tail()
        if not sample:
            car_a[j, :, cols[c]] = conv_a[c].tail()
            car_g[j, :, cols[c]] = conv_g[c].tail()

    if not fused_ln:
        @pl.when(j == pl.num_programs(1) - 1)
        def _():
            y_ref[...] = _layer_norm(DEEPNORM_ALPHA * x_ref[...] + acc[...], g_ref[...], b_ref[...])


def _conv_ffn(x, layer, w, state=None, seq_len=None, stride=1):
    t = x.shape[0]
    sample = state is not None
    tm = t if sample else FF_TM
    n_tiles = t // tm
    halo_rows = (CONV_W - 1) * stride if sample else HALO_PROMPT
    kc = 1 if sample else N_FF_CHUNKS
    nj = N_FF_CHUNKS // kc
    width = kc * FF_CHUNK
    resident = dict(pipeline_mode=pl.Buffered(1)) if nj == 1 else {}
    in_specs = [
        pl.BlockSpec((tm, D_MODEL), lambda i, j: (i, 0)),
        pl.BlockSpec((None, D_MODEL, width), lambda i, j: (layer, 0, j), **resident),
        pl.BlockSpec((None, D_MODEL, width), lambda i, j: (layer, 0, nj + j), **resident),
        pl.BlockSpec((None, CONV_W, width), lambda i, j: (layer, 0, j)),
        pl.BlockSpec((None, CONV_W, width), lambda i, j: (layer, 0, nj + j)),
        pl.BlockSpec((None, 1, width), lambda i, j: (layer, 0, j)),
        pl.BlockSpec((None, 1, width), lambda i, j: (layer, 0, nj + j)),
        pl.BlockSpec((None, width, D_MODEL), lambda i, j: (layer, j, 0), **resident),
        pl.BlockSpec((None, 1, D_MODEL), lambda i, j: (2 * layer + 1, 0, 0)),
        pl.BlockSpec((None, 1, D_MODEL), lambda i, j: (2 * layer + 1, 0, 0)),
    ]
    args = [x, w['w_up'], w['w_up'], w['w_fconv'], w['w_fconv'], w['b_fconv'], w['b_fconv'],
            w['w_down'], w['ln_g'], w['ln_b']]
    scratch = [] if nj == 1 else [pltpu.VMEM((tm, D_MODEL), F32)]
    if sample:
        in_specs += [pl.BlockSpec((halo_rows, width), lambda i, j: (0, j)),
                     pl.BlockSpec((halo_rows, width), lambda i, j: (0, nj + j))]
        args += [state, state]
    else:
        scratch += [pltpu.VMEM((nj, halo_rows, width), F32), pltpu.VMEM((nj, halo_rows, width), F32)]
    out_shape = [jax.ShapeDtypeStruct((t, D_MODEL), F32),
                 jax.ShapeDtypeStruct((n_tiles, halo_rows, D_FF), F32),
                 jax.ShapeDtypeStruct((n_tiles, halo_rows, D_FF), F32)]
    out_specs = [pl.BlockSpec((tm, D_MODEL), lambda i, j: (i, 0)),
                 pl.BlockSpec((None, halo_rows, width), lambda i, j: (i, 0, j)),
                 pl.BlockSpec((None, halo_rows, width), lambda i, j: (i, 0, j))]
    return pl.pallas_call(
        functools.partial(_ffn_kernel, None if sample else seq_len // tm, stride, kc),
        grid=(n_tiles, nj), in_specs=in_specs, out_specs=out_specs, out_shape=out_shape,
        scratch_shapes=scratch,
        compiler_params=_cparams(("arbitrary", "arbitrary")), name="conv_ffn",
    )(*args)


def _sconv_kernel(tiles_per_seq, stride, x_ref, win_ref, wc_ref, wout_ref, g_ref, b_ref, *rest):
    sample = tiles_per_seq is None
    if sample:
        s_ref, y_ref, last_ref = rest
        conv = _CausalConv(_state_blocks(s_ref, stride), stride)
    else:
        y_ref, last_ref, car = rest

        @pl.when((pl.program_id(0) % tiles_per_seq) == 0)
        def _():
            car[...] = jnp.zeros(car.shape, F32)

        conv = _CausalConv(car[...], stride)

    def proj_in(r):
        return _dot(x_ref[r * ROW_BLOCK:(r + 1) * ROW_BLOCK, :].astype(BF16), win_ref[...])

    def mix_out(r, bch):
        rows = slice(r * ROW_BLOCK, (r + 1) * ROW_BLOCK)
        g_b = bch[:, :D_MODEL]
        u = bch[:, D_MODEL:2 * D_MODEL] * bch[:, 2 * D_MODEL:]
        mix = _dot((g_b * conv(u, wc_ref)).astype(BF16), wout_ref[...])
        y_ref[rows, :] = _layer_norm(DEEPNORM_ALPHA * x_ref[rows, :] + mix, g_ref[...], b_ref[...])

    _software_pipeline(x_ref.shape[0] // ROW_BLOCK, 1, proj_in, mix_out)

    last_ref[...] = conv.tail()
    if not sample:
        car[...] = conv.tail()


def _short_conv(x, w, state=None, tiles_per_seq=None, stride=1):
    t = x.shape[0]
    n_tiles = t // TM
    sample = state is not None
    halo_rows = (CONV_W - 1) * stride if sample else HALO_PROMPT
    const = lambda i: (0, 0)
    row = lambda i: (i, 0)
    in_specs = [pl.BlockSpec((TM, D_MODEL), row), pl.BlockSpec(w['w_b_in'].shape, const),
                pl.BlockSpec((CONV_W, D_MODEL), const), pl.BlockSpec(w['w_b_out'].shape, const),
                pl.BlockSpec((None, 1, D_MODEL), lambda i: (2, 0, 0)),
                pl.BlockSpec((None, 1, D_MODEL), lambda i: (2, 0, 0))]
    args = [x, w['w_b_in'], w['w_b_conv'], w['w_b_out'], w['ln_g'], w['ln_b']]
    scratch = []
    if sample:
        in_specs += [pl.BlockSpec((halo_rows, D_MODEL), const)]
        args += [state]
    else:
        scratch += [pltpu.VMEM((halo_rows, D_MODEL), F32)]
    return pl.pallas_call(
        functools.partial(_sconv_kernel, None if sample else tiles_per_seq, stride),
        grid=(n_tiles,), in_specs=in_specs,
        out_specs=[pl.BlockSpec((TM, D_MODEL), row),
                   pl.BlockSpec((None, halo_rows, D_MODEL), lambda i: (i, 0, 0))],
        out_shape=[jax.ShapeDtypeStruct((t, D_MODEL), F32),
                   jax.ShapeDtypeStruct((n_tiles, halo_rows, D_MODEL), F32)],
        scratch_shapes=scratch,
        compiler_params=_cparams(("arbitrary",)), name="short_conv",
    )(*args)


def _rot_cols(w):
    return jnp.concatenate([-w[..., HALF_ROPE:], w[..., :HALF_ROPE]], axis=-1)


def _prepare_weights(w_a_down, g_q_norm, g_kv_norm, w_a_uq, w_a_uk, w_a_uv, w_a_out,
                     w_b_in, w_b_conv, w_b_out, w_f_up, w_f_conv, b_f_conv, w_f_down, ln_g, ln_b):
    o = Q_LORA + KV_LORA
    w_kr = w_a_down[:, o:]
    zpad = jnp.zeros((D_MODEL, LANES - QK_ROPE), F32)
    wd = jnp.concatenate([w_a_down[:, :o], w_kr, zpad, _rot_cols(w_kr), zpad], axis=1)

    wq3 = w_a_uq.reshape(Q_LORA, N_HEADS, QK_HEAD)
    zq = jnp.zeros((Q_LORA, N_HEADS, HEAD_PAD - QK_HEAD), F32)
    q_main = jnp.concatenate([wq3, zq], axis=-1)
    q_rot = jnp.concatenate([jnp.zeros((Q_LORA, N_HEADS, QK_NOPE), F32), _rot_cols(wq3[..., QK_NOPE:]), zq],
                            axis=-1)
    wq = jnp.concatenate([q_main.reshape(Q_LORA, -1), q_rot.reshape(Q_LORA, -1)], axis=1)

    wk = jnp.concatenate([w_a_uk, jnp.zeros((KV_LORA, N_HEADS, HEAD_PAD - QK_NOPE), F32)], axis=-1)
    wk = wk.reshape(KV_LORA, N_HEADS * HEAD_PAD)
    src = jnp.arange(LANES)[:, None]
    dst = jnp.arange(N_HEADS * HEAD_PAD)[None, :]
    e = ((src < QK_ROPE) & ((dst % HEAD_PAD) == QK_NOPE + src)).astype(BF16)

    w_ukt = jnp.transpose(w_a_uk, (1, 2, 0))
    w_ukt = jnp.concatenate([w_ukt, jnp.zeros((N_HEADS, HEAD_PAD - QK_NOPE, KV_LORA), F32)], axis=1)
    w_uvh = jnp.transpose(w_a_uv, (1, 0, 2))
    zv = jnp.zeros_like(w_uvh)
    even = (jnp.arange(N_HEADS) % 2 == 0)[:, None, None]
    w_uv_pair = jnp.where(even, jnp.concatenate([w_uvh, zv], -1), jnp.concatenate([zv, w_uvh], -1))

    return dict(
        wd=wd.astype(BF16), gq=g_q_norm.reshape(1, -1), gkv=g_kv_norm.reshape(1, -1), wq=wq.astype(BF16),
        wk=wk.T.astype(BF16), e=e.T, wv=w_a_uv.reshape(KV_LORA, -1).astype(BF16),
        w_ukt=w_ukt.astype(BF16), w_uv_pair=w_uv_pair.astype(BF16), w_a_out=w_a_out.astype(BF16),
        w_b_in=w_b_in.astype(BF16), w_b_conv=w_b_conv, w_b_out=w_b_out.astype(BF16),
        w_up=w_f_up.astype(BF16), w_fconv=w_f_conv, b_fconv=b_f_conv.reshape(DEPTH, 1, 2 * D_FF),
        w_down=w_f_down.astype(BF16), ln_g=ln_g.reshape(2 * DEPTH, 1, D_MODEL),
        ln_b=ln_b.reshape(2 * DEPTH, 1, D_MODEL),
    )


def _rope_tables(pos):
    inv = 1.0 / (ROPE_THETA ** (jnp.arange(0, QK_ROPE, 2, dtype=F32) / QK_ROPE))
    ang = pos.astype(F32)[:, None] * inv[None, :]
    cos, sin = jnp.cos(ang), jnp.sin(ang)
    p = pos.shape[0]
    one = jnp.ones((p, QK_NOPE), F32)
    z64 = jnp.zeros((p, QK_NOPE), F32)
    z32 = jnp.zeros((p, HEAD_PAD - QK_HEAD), F32)
    z96 = jnp.zeros((p, LANES - QK_ROPE), F32)
    cosg = jnp.concatenate([one, cos, cos, z32], axis=1)
    sing = jnp.concatenate([z64, sin, sin, z32], axis=1)
    ckr = jnp.concatenate([cos, cos, z96], axis=1)
    skr = jnp.concatenate([sin, sin, z96], axis=1)
    return cosg, sing, ckr, skr


def _prompt_trunk(x_prompt, w):
    b, s, _ = x_prompt.shape
    x = x_prompt.reshape(b * s, D_MODEL)
    tps = s // TM
    tabs = _rope_tables(jnp.arange(s))
    q, c_kv, k_r, k, v = _mla_proj(x, tabs, w, True, tps)
    o = _flash_attention(q, k, v, b, s)
    x = _out_proj_ln(x, o, w['w_a_out'], w['ln_g'][0], w['ln_b'][0])
    x, la0, lg0 = _conv_ffn(x, 0, w, seq_len=s)
    x, lc = _short_conv(x, w, tiles_per_seq=tps)
    x, la1, lg1 = _conv_ffn(x, 1, w, seq_len=s)

    def seq_last(a, tiles=tps):
        return a[tiles - 1::tiles, HALO_PROMPT - (CONV_W - 1):, :]

    ffn_last = lambda a: seq_last(a, s // FF_TM)
    ffn_state = jnp.stack([jnp.concatenate([ffn_last(la0), ffn_last(lg0)], axis=-1),
                           jnp.concatenate([ffn_last(la1), ffn_last(lg1)], axis=-1)])
    return (x.reshape(b, s, D_MODEL),
            c_kv.reshape(b, s // PAGE_SIZE, PAGE_SIZE, KV_LORA),
            k_r.reshape(b, s // PAGE_SIZE, PAGE_SIZE, QK_ROPE),
            seq_last(lc), ffn_state)


def _sample_trunk(x_sample, cache_c, cache_r, state_conv_b, state_ffn_conv, page_table, w):
    b, t, _ = x_sample.shape
    n = b * t
    past_len = page_table.shape[1] * cache_c.shape[1]
    tm_rows = lambda a: jnp.swapaxes(a, 0, 1).reshape(a.shape[1] * a.shape[0], a.shape[-1])
    bm_rows = lambda a, k: jnp.swapaxes(a.reshape(k, b, a.shape[-1]), 0, 1)
    x = tm_rows(x_sample)
    pos = past_len + jnp.repeat(jnp.arange(t), b)
    tabs = _rope_tables(pos)
    q, c_kv, k_r = _mla_proj(x, tabs, w, False, n // TM)

    q_lat = _absorb_q(q, w['w_ukt'])
    q_lat = jnp.transpose(q_lat.reshape(N_HEADS, t, b, KV_LORA), (2, 1, 0, 3)).reshape(b, t * N_HEADS, KV_LORA)
    q_rope = q.reshape(t, b, N_HEADS, HEAD_PAD)[..., QK_NOPE:QK_HEAD]
    q_rope = jnp.transpose(q_rope, (1, 0, 2, 3)).reshape(b, t * N_HEADS, QK_ROPE)
    pad_new = lambda a: jnp.pad(bm_rows(a, t), ((0, 0), (0, SUBLANES - t), (0, 0)))
    cache_rt = jnp.swapaxes(cache_r, 1, 2)
    o_lat = _decode_attention(q_lat, q_rope, pad_new(c_kv), pad_new(k_r), cache_c, cache_rt, page_table, t)
    o_lat = jnp.transpose(o_lat.reshape(b, t, N_HEADS, KV_LORA), (2, 1, 0, 3)).reshape(N_HEADS, n, KV_LORA)
    o = _unabsorb_o(o_lat.astype(BF16), w['w_uv_pair'])

    x = _out_proj_ln(x, o, w['w_a_out'], w['ln_g'][0], w['ln_b'][0])
    x, la0, lg0 = _conv_ffn(x, 0, w, state=tm_rows(state_ffn_conv[0]), stride=b)
    x, lc = _short_conv(x, w, state=tm_rows(state_conv_b), stride=b)
    x, la1, lg1 = _conv_ffn(x, 1, w, state=tm_rows(state_ffn_conv[1]), stride=b)

    k = CONV_W - 1
    ffn_state = jnp.stack([bm_rows(jnp.concatenate([la0[0], lg0[0]], axis=-1), k),
                           bm_rows(jnp.concatenate([la1[0], lg1[0]], axis=-1), k)])
    return (bm_rows(x, t), bm_rows(c_kv, t), bm_rows(k_r, t), bm_rows(lc[0], k), ffn_state)


def kernel(x_prompt, x_sample, cache_kv_latent, cache_k_rope, state_conv_b, state_ffn_conv, page_table,
           w_a_down, g_q_norm, g_kv_norm, w_a_uq, w_a_uk, w_a_uv, w_a_out,
           w_b_in, w_b_conv, w_b_out, w_f_up, w_f_conv, b_f_conv, w_f_down, ln_g, ln_b):
    w = _prepare_weights(w_a_down, g_q_norm, g_kv_norm, w_a_uq, w_a_uk, w_a_uv, w_a_out,
                         w_b_in, w_b_conv, w_b_out, w_f_up, w_f_conv, b_f_conv, w_f_down, ln_g, ln_b)
    y_p, lat_p, rope_p, conv_p, ffn_p = _prompt_trunk(x_prompt, w)
    y_s, lat_s, rope_s, conv_s, ffn_s = _sample_trunk(
        x_sample, cache_kv_latent, cache_k_rope, state_conv_b, state_ffn_conv, page_table, w)
    return (y_p, y_s, lat_p, rope_p, lat_s, rope_s, conv_p, conv_s, ffn_p, ffn_s)
```

```python
import functools
import math

import jax
import jax.numpy as jnp
from jax import lax
from jax.experimental import pallas as pl
from jax.experimental.pallas import tpu as pltpu

D_MODEL = 1024
N_HEADS = 16
Q_LORA = 384
KV_LORA = 256
QK_NOPE = 64
QK_ROPE = 32
HALF_ROPE = QK_ROPE // 2
V_HEAD = 64
QK_HEAD = QK_NOPE + QK_ROPE
ROPE_THETA = 10000.0
ATTN_SCALE = QK_HEAD ** -0.5
Q_SCALE = ATTN_SCALE * math.log2(math.e)
CONV_W = 3
D_FF = 2816
RMS_EPS = 1e-6
LN_EPS = 1e-5
DEPTH = 2
DEEPNORM_ALPHA = (2 * DEPTH) ** 0.25
PAGE_SIZE = 128

LANES = 128
SUBLANES = 8
HEAD_PAD = LANES
VMEM_LIMIT = 56 * 1024 * 1024

TM = 512
FF_TM = 1024
TQ = 512
TK = 512
FLASH_HEADS = 8
RQ = 128
QK_AHEAD = 4
FF_CHUNK = 1408
N_FF_CHUNKS = D_FF // FF_CHUNK
FF_ROWS = 128
ROW_BLOCK = 128
MLA_ROWS = 256
UP_AHEAD = 2
PAGES_PER_GROUP = 16
DECODE_SLOTS = 8
HALO_PROMPT = SUBLANES

BF16 = jnp.bfloat16
F32 = jnp.float32


def _dot(a, b):
    return jnp.dot(a, b, preferred_element_type=F32)


def _dot_nt(a, b):
    return lax.dot_general(a, b, (((1,), (1,)), ((), ())), preferred_element_type=F32)


def _layer_norm(v, g, b):
    mu = jnp.mean(v, axis=-1, keepdims=True)
    vc = v - mu
    var = jnp.mean(vc * vc, axis=-1, keepdims=True)
    return vc * lax.rsqrt(var + LN_EPS) * g + b


def _rms_norm(v, g):
    return v * lax.rsqrt(jnp.mean(v * v, axis=-1, keepdims=True) + RMS_EPS) * g


def _software_pipeline(n_blocks, ahead, first_stage, second_stage):
    pending = [first_stage(r) for r in range(min(ahead, n_blocks))]
    for r in range(n_blocks):
        value = pending.pop(0)
        if r + ahead < n_blocks:
            pending.append(first_stage(r + ahead))
        second_stage(r, value)


def _cparams(sem):
    return pltpu.CompilerParams(dimension_semantics=sem, vmem_limit_bytes=VMEM_LIMIT)


def _mla_proj_kernel(emit_kv, x_ref, wd_ref, gq_ref, gkv_ref, wq_ref, cosg_ref, sing_ref,
                     ckr_ref, skr_ref, *rest):
    if emit_kv:
        wk_ref, e_ref, wv_ref, q_ref, ckv_ref, kr_ref, k_ref, v_ref = rest
    else:
        q_ref, ckv_ref, kr_ref = rest
    qw = N_HEADS * HEAD_PAD
    o = Q_LORA + KV_LORA

    def down(r):
        rows = slice(r * MLA_ROWS, (r + 1) * MLA_ROWS)
        return _dot(x_ref[rows, :].astype(BF16), wd_ref[...])

    def rest_of(r, d):
        rows = slice(r * MLA_ROWS, (r + 1) * MLA_ROWS)
        c_q = _rms_norm(d[:, :Q_LORA], gq_ref[...])
        c_kv = _rms_norm(d[:, Q_LORA:o], gkv_ref[...])
        ckv_ref[rows, :] = c_kv
        krg = d[:, o:o + LANES] * ckr_ref[rows, :] + d[:, o + LANES:o + 2 * LANES] * skr_ref[rows, :]
        kr_ref[rows, :] = krg[:, :QK_ROPE]
        q2 = _dot(c_q.astype(BF16), wq_ref[...])
        cosg = cosg_ref[rows, :] * Q_SCALE
        sing = sing_ref[rows, :] * Q_SCALE
        for h in range(N_HEADS):
            lo = h * HEAD_PAD
            q_ref[rows, lo:lo + HEAD_PAD] = (q2[:, lo:lo + HEAD_PAD] * cosg
                                             + q2[:, qw + lo:qw + lo + HEAD_PAD] * sing).astype(BF16)
        if emit_kv:
            ckvb = c_kv.astype(BF16)
            kt = _dot_nt(wk_ref[...], ckvb) + _dot_nt(e_ref[...], krg.astype(BF16))
            k_ref[:, rows] = kt.astype(BF16)
            v_ref[rows, :] = _dot(ckvb, wv_ref[...]).astype(BF16)

    _software_pipeline(x_ref.shape[0] // MLA_ROWS, 1, down, rest_of)


def _mla_proj(x, tabs, w, emit_kv, pos_tiles):
    t = x.shape[0]
    n = t // TM
    qw = N_HEADS * HEAD_PAD
    const = lambda i: (0, 0)
    row = lambda i: (i, 0)
    tab = lambda i: (i % pos_tiles, 0)
    in_specs = [
        pl.BlockSpec((TM, D_MODEL), row),
        pl.BlockSpec(w['wd'].shape, const),
        pl.BlockSpec((1, Q_LORA), const),
        pl.BlockSpec((1, KV_LORA), const),
        pl.BlockSpec(w['wq'].shape, const),
    ] + [pl.BlockSpec((TM, LANES), tab)] * 4
    args = [x, w['wd'], w['gq'], w['gkv'], w['wq']] + list(tabs)
    out_shape = [jax.ShapeDtypeStruct((t, qw), BF16),
                 jax.ShapeDtypeStruct((t, KV_LORA), F32),
                 jax.ShapeDtypeStruct((t, QK_ROPE), F32)]
    out_specs = [pl.BlockSpec((TM, qw), row), pl.BlockSpec((TM, KV_LORA), row),
                 pl.BlockSpec((TM, QK_ROPE), row)]
    if emit_kv:
        in_specs += [pl.BlockSpec(w['wk'].shape, const), pl.BlockSpec(w['e'].shape, const),
                     pl.BlockSpec(w['wv'].shape, const)]
        args += [w['wk'], w['e'], w['wv']]
        out_shape += [jax.ShapeDtypeStruct((qw, t), BF16),
                      jax.ShapeDtypeStruct((t, N_HEADS * V_HEAD), BF16)]
        out_specs += [pl.BlockSpec((qw, TM), lambda i: (0, i)),
                      pl.BlockSpec((TM, N_HEADS * V_HEAD), row)]
    return pl.pallas_call(
        functools.partial(_mla_proj_kernel, emit_kv),
        grid=(n,), in_specs=in_specs, out_specs=out_specs, out_shape=out_shape,
        compiler_params=_cparams(("parallel",)), name="mla_proj_kv" if emit_kv else "mla_proj",
    )(*args)


def _flash_kernel(qi_tab, ki_tab, q_ref, k_ref, v_ref, o_ref, m_sc, l_sc, acc_sc):
    step = pl.program_id(2)
    qi = qi_tab[step]
    ki = ki_tab[step]

    @pl.when(ki == 0)
    def _():
        m_sc[...] = jnp.full(m_sc.shape, -jnp.inf, F32)
        l_sc[...] = jnp.zeros(l_sc.shape, F32)
        acc_sc[...] = jnp.zeros(acc_sc.shape, F32)

    def tile(diagonal):
        chains = [(hh, r) for hh in range(FLASH_HEADS) for r in range(TQ // RQ)]

        def n_keys(r):
            return (r + 1) * RQ if diagonal else TK

        def scores(chain):
            hh, r = chain
            hs = slice(hh * HEAD_PAD, (hh + 1) * HEAD_PAD)
            return _dot(q_ref[r * RQ:(r + 1) * RQ, hs], k_ref[hs, 0:n_keys(r)])

        def finish(chain, s):
            hh, r = chain
            rows = slice(r * RQ, (r + 1) * RQ)
            nk = n_keys(r)
            if diagonal:
                row = lax.broadcasted_iota(jnp.int32, (RQ, nk), 0) + r * RQ
                col = lax.broadcasted_iota(jnp.int32, (RQ, nk), 1)
                s = jnp.where(col <= row, s, -jnp.inf)
            chunks = [s[:, c * LANES:(c + 1) * LANES] for c in range(nk // LANES)]
            mx = chunks[0]
            for c in chunks[1:]:
                mx = jnp.maximum(mx, c)
            m_prev = m_sc[hh, rows, :]
            m_new = jnp.maximum(m_prev, jnp.max(mx, axis=-1, keepdims=True))
            alpha = jnp.exp2(m_prev - m_new)
            ps = [jnp.exp2(c - m_new) for c in chunks]
            psum = ps[0]
            for pc in ps[1:]:
                psum = psum + pc
            p = jnp.concatenate(ps, axis=1).astype(BF16) if len(ps) > 1 else ps[0].astype(BF16)
            l_sc[hh, rows, :] = alpha * l_sc[hh, rows, :] + psum
            vs = slice((hh // 2) * 2 * V_HEAD, (hh // 2 + 1) * 2 * V_HEAD)
            acc_sc[hh, rows, :] = alpha * acc_sc[hh, rows, :] + _dot(p, v_ref[0:nk, vs])
            m_sc[hh, rows, :] = m_new

        _software_pipeline(len(chains), QK_AHEAD, lambda i: scores(chains[i]),
                           lambda i, s: finish(chains[i], s))

    @pl.when(ki < qi)
    def _():
        tile(False)

    @pl.when(ki == qi)
    def _():
        tile(True)
        lane = lax.broadcasted_iota(jnp.int32, (TQ, 2 * V_HEAD), 1)
        for pair in range(FLASH_HEADS // 2):
            o0, o1 = [acc_sc[h] / jnp.sum(l_sc[h], axis=-1, keepdims=True) for h in (2 * pair, 2 * pair + 1)]
            o_ref[:, pair * 2 * V_HEAD:(pair + 1) * 2 * V_HEAD] = jnp.where(lane < V_HEAD, o0, o1).astype(BF16)


def _flash_attention(q, k, v, batch, seq):
    nq = seq // TQ
    pairs = [(a, b) for a in range(nq) for b in range(a + 1)]
    qi_tab = jnp.asarray([p[0] for p in pairs], jnp.int32)
    ki_tab = jnp.asarray([p[1] for p in pairs], jnp.int32)
    grid_spec = pltpu.PrefetchScalarGridSpec(
        num_scalar_prefetch=2,
        grid=(batch, N_HEADS // FLASH_HEADS, len(pairs)),
        in_specs=[
            pl.BlockSpec((TQ, FLASH_HEADS * HEAD_PAD), lambda b, h, s, qt, kt: (b * nq + qt[s], h)),
            pl.BlockSpec((FLASH_HEADS * HEAD_PAD, TK), lambda b, h, s, qt, kt: (h, b * nq + kt[s])),
            pl.BlockSpec((TK, FLASH_HEADS * V_HEAD), lambda b, h, s, qt, kt: (b * nq + kt[s], h)),
        ],
        out_specs=pl.BlockSpec((TQ, FLASH_HEADS * V_HEAD), lambda b, h, s, qt, kt: (b * nq + qt[s], h)),
        scratch_shapes=[pltpu.VMEM((FLASH_HEADS, TQ, LANES), F32), pltpu.VMEM((FLASH_HEADS, TQ, LANES), F32),
                        pltpu.VMEM((FLASH_HEADS, TQ, 2 * V_HEAD), F32)],
    )
    return pl.pallas_call(
        _flash_kernel, grid_spec=grid_spec,
        out_shape=jax.ShapeDtypeStruct((batch * seq, N_HEADS * V_HEAD), BF16),
        compiler_params=_cparams(("parallel", "parallel", "arbitrary")), name="flash_attn",
    )(qi_tab, ki_tab, q, k, v)


def _absorb_kernel(q_ref, w_ref, o_ref):
    o_ref[...] = _dot(q_ref[...], w_ref[...]).astype(BF16)


def _absorb_q(q, w_ukt):
    t = q.shape[0]
    return pl.pallas_call(
        _absorb_kernel, grid=(N_HEADS,),
        in_specs=[pl.BlockSpec((t, HEAD_PAD), lambda h: (0, h)),
                  pl.BlockSpec((None, HEAD_PAD, KV_LORA), lambda h: (h, 0, 0))],
        out_specs=pl.BlockSpec((None, t, KV_LORA), lambda h: (h, 0, 0)),
        out_shape=jax.ShapeDtypeStruct((N_HEADS, t, KV_LORA), BF16),
        compiler_params=_cparams(("parallel",)), name="absorb_q",
    )(q, w_ukt)


def _unabsorb_kernel(o_ref, w_ref, out_ref):
    out_ref[...] = (_dot(o_ref[0], w_ref[0]) + _dot(o_ref[1], w_ref[1])).astype(BF16)


def _unabsorb_o(o_lat, w_uv_pair):
    t = o_lat.shape[1]
    return pl.pallas_call(
        _unabsorb_kernel, grid=(N_HEADS // 2,),
        in_specs=[pl.BlockSpec((2, t, KV_LORA), lambda j: (j, 0, 0)),
                  pl.BlockSpec((2, KV_LORA, 2 * V_HEAD), lambda j: (j, 0, 0))],
        out_specs=pl.BlockSpec((t, 2 * V_HEAD), lambda j: (0, j)),
        out_shape=jax.ShapeDtypeStruct((t, N_HEADS * V_HEAD), BF16),
        compiler_params=_cparams(("parallel",)), name="unabsorb_o",
    )(o_lat, w_uv_pair)


def _decode_kernel(n_new, n_groups, pt_ref, ql_ref, qr_ref, cn_ref, rn_ref, cache_c, cache_r, o_ref,
                   cbuf, rbuf, ctbuf, sem_c, sem_r):
    b = pl.program_id(0)
    rows = ql_ref.shape[0]
    gk = PAGES_PER_GROUP * PAGE_SIZE

    def group_copies(seq, g):
        slot = g % DECODE_SLOTS
        copies = []
        for j in range(PAGES_PER_GROUP):
            page = pt_ref[seq, g * PAGES_PER_GROUP + j]
            keys = pl.ds(j * PAGE_SIZE, PAGE_SIZE)
            copies.append(pltpu.make_async_copy(cache_c.at[page], cbuf.at[slot, keys, :], sem_c.at[slot]))
            copies.append(pltpu.make_async_copy(cache_r.at[page], rbuf.at[slot, :, keys], sem_r.at[slot]))
        return copies

    def start(seq, g):
        for cp in group_copies(seq, g):
            cp.start()

    def wait(seq, g):
        for cp in group_copies(seq, g):
            cp.wait()

    @pl.when(b == 0)
    def _():
        for g in range(DECODE_SLOTS):
            start(0, g)

    ql = ql_ref[...]
    qr = qr_ref[...]

    def prepare(g):
        c = cbuf[g % DECODE_SLOTS].astype(BF16)
        ctbuf[g % 2] = c.T
        return c

    def scores(g):
        return _dot(ql, ctbuf[g % 2]) + _dot(qr, rbuf[g % DECODE_SLOTS].astype(BF16))

    m = jnp.full((rows, LANES), -jnp.inf, F32)
    l = jnp.zeros((rows, LANES), F32)
    acc = jnp.zeros((rows, KV_LORA), F32)
    wait(b, 0)
    wait(b, 1)
    c_cur = prepare(0)
    s_cur = scores(0)
    c_nxt = prepare(1)
    for g in range(n_groups):
        s, c = s_cur, c_cur
        if g + 2 < n_groups:
            wait(b, g + 2)
        if g + 1 < n_groups:
            s_cur, c_cur = scores(g + 1), c_nxt
        if g + 2 < n_groups:
            c_nxt = prepare(g + 2)
        chunks = [s[:, k * LANES:(k + 1) * LANES] for k in range(gk // LANES)]
        mx = chunks[0]
        for ch in chunks[1:]:
            mx = jnp.maximum(mx, ch)
        m_new = jnp.maximum(m, jnp.max(mx, axis=-1, keepdims=True))
        alpha = jnp.exp2(m - m_new)
        ps = [jnp.exp2(ch - m_new) for ch in chunks]
        psum = ps[0]
        for pc in ps[1:]:
            psum = psum + pc
        l = alpha * l + psum
        acc = alpha[:, 0:1] * acc + _dot(jnp.concatenate(ps, axis=1).astype(BF16), c)
        m = m_new
        if g + DECODE_SLOTS < n_groups:
            start(b, g + DECODE_SLOTS)
        else:
            start(b + 1, g + DECODE_SLOTS - n_groups)

    @pl.when(b == pl.num_programs(0) - 1)
    def _():
        for g in range(DECODE_SLOTS):
            wait(b + 1, g)

    qlf = ql.astype(F32)
    qrf = qr.astype(F32)
    row = lax.broadcasted_iota(jnp.int32, (rows, 1), 0)
    s_new = []
    for j in range(n_new):
        sj = (jnp.sum(qlf * cn_ref[j:j + 1, :], axis=-1, keepdims=True)
              + jnp.sum(qrf * rn_ref[j:j + 1, :], axis=-1, keepdims=True))
        s_new.append(jnp.where(row >= j * N_HEADS, sj, -jnp.inf))
    m_old = m[:, 0:1]
    m_fin = m_old
    for sj in s_new:
        m_fin = jnp.maximum(m_fin, sj)
    a = jnp.exp2(m_old - m_fin)
    l_fin = a * jnp.sum(l, axis=-1, keepdims=True)
    acc = a * acc
    for j in range(n_new):
        pj = jnp.exp2(s_new[j] - m_fin)
        l_fin = l_fin + pj
        acc = acc + pj * cn_ref[j:j + 1, :]
    o_ref[...] = acc / l_fin


def _decode_attention(q_lat, q_rope, c_new, r_new, cache_c, cache_r, page_table, n_new):
    b, rows, _ = q_lat.shape
    n_pages = page_table.shape[1]
    assert n_pages % PAGES_PER_GROUP == 0
    n_groups = n_pages // PAGES_PER_GROUP
    assert n_groups % DECODE_SLOTS == 0
    gk = PAGES_PER_GROUP * PAGE_SIZE
    seq = lambda i, pt: (i, 0, 0)
    any_spec = pl.BlockSpec(memory_space=pl.ANY)
    grid_spec = pltpu.PrefetchScalarGridSpec(
        num_scalar_prefetch=1, grid=(b,),
        in_specs=[pl.BlockSpec((None, rows, KV_LORA), seq), pl.BlockSpec((None, rows, QK_ROPE), seq),
                  pl.BlockSpec((None, SUBLANES, KV_LORA), seq), pl.BlockSpec((None, SUBLANES, QK_ROPE), seq),
                  any_spec, any_spec],
        out_specs=pl.BlockSpec((None, rows, KV_LORA), seq),
        scratch_shapes=[pltpu.VMEM((DECODE_SLOTS, gk, KV_LORA), F32),
                        pltpu.VMEM((DECODE_SLOTS, QK_ROPE, gk), F32),
                        pltpu.VMEM((2, KV_LORA, gk), BF16),
                        pltpu.SemaphoreType.DMA((DECODE_SLOTS,)),
                        pltpu.SemaphoreType.DMA((DECODE_SLOTS,))],
    )
    return pl.pallas_call(
        functools.partial(_decode_kernel, n_new, n_groups), grid_spec=grid_spec,
        out_shape=jax.ShapeDtypeStruct((b, rows, KV_LORA), F32),
        compiler_params=_cparams(("arbitrary",)), name="decode_attn",
    )(jnp.concatenate([page_table, page_table[:1]], axis=0), q_lat, q_rope, c_new, r_new, cache_c, cache_r)


def _out_proj_kernel(x_ref, o_ref, w_ref, g_ref, b_ref, y_ref):
    def proj(r):
        return _dot(o_ref[r * ROW_BLOCK:(r + 1) * ROW_BLOCK, :], w_ref[...])

    def norm(r, mix):
        rows = slice(r * ROW_BLOCK, (r + 1) * ROW_BLOCK)
        y_ref[rows, :] = _layer_norm(DEEPNORM_ALPHA * x_ref[rows, :] + mix, g_ref[...], b_ref[...])

    _software_pipeline(x_ref.shape[0] // ROW_BLOCK, 1, proj, norm)


def _out_proj_ln(x, o, w, g, b):
    t = x.shape[0]
    const = lambda i: (0, 0)
    row = lambda i: (i, 0)
    return pl.pallas_call(
        _out_proj_kernel, grid=(t // TM,),
        in_specs=[pl.BlockSpec((TM, D_MODEL), row), pl.BlockSpec((TM, N_HEADS * V_HEAD), row),
                  pl.BlockSpec(w.shape, const), pl.BlockSpec((1, D_MODEL), const),
                  pl.BlockSpec((1, D_MODEL), const)],
        out_specs=pl.BlockSpec((TM, D_MODEL), row),
        out_shape=jax.ShapeDtypeStruct((t, D_MODEL), F32),
        compiler_params=_cparams(("parallel",)), name="attn_out_ln",
    )(x, o, w, g, b)


class _CausalConv:
    def __init__(self, history, stride):
        self.history = history
        self.stride = stride

    def __call__(self, u, w_ref):
        n = u.shape[0]
        y = w_ref[CONV_W - 1:CONV_W, :] * u
        if self.stride == 1:
            z = jnp.concatenate([self.history, u], axis=0).reshape(n // SUBLANES + 1, SUBLANES, -1)
            sub = lax.broadcasted_iota(jnp.int32, (n // SUBLANES, SUBLANES, u.shape[1]), 1)
            for j in range(CONV_W - 1):
                back = CONV_W - 1 - j
                rot = pltpu.roll(z, back, axis=1)
                shifted = jnp.where(sub < back, rot[:-1], rot[1:]).reshape(n, -1)
                y = y + w_ref[j:j + 1, :] * shifted
            self.history = u[n - SUBLANES:, :]
        else:
            assert n == self.stride
            for j in range(CONV_W - 1):
                y = y + w_ref[j:j + 1, :] * self.history[j]
            self.history = self.history[1:] + [u]
        return y

    def tail(self):
        return self.history if self.stride == 1 else jnp.concatenate(self.history, axis=0)


def _state_blocks(ref, stride):
    return [ref[j * stride:(j + 1) * stride, :] for j in range(CONV_W - 1)]


def _ffn_kernel(tiles_per_seq, stride, kc, x_ref, wua_ref, wug_ref, wca_ref, wcg_ref,
                ba_ref, bg_ref, wd_ref, g_ref, b_ref, *rest):
    sample = tiles_per_seq is None
    fused_ln = kc == N_FF_CHUNKS
    j = pl.program_id(1)
    cols = [slice(c * FF_CHUNK, (c + 1) * FF_CHUNK) for c in range(kc)]
    if sample:
        sa_ref, sg_ref, y_ref, la_ref, lg_ref = rest[:5]
        conv_a = [_CausalConv(_state_blocks(sa_ref.at[:, cs], stride), stride) for cs in cols]
        conv_g = [_CausalConv(_state_blocks(sg_ref.at[:, cs], stride), stride) for cs in cols]
    else:
        y_ref, la_ref, lg_ref = rest[:3]
        car_a, car_g = rest[-2:]

        @pl.when((pl.program_id(0) % tiles_per_seq) == 0)
        def _():
            car_a[j] = jnp.zeros(car_a.shape[1:], F32)
            car_g[j] = jnp.zeros(car_g.shape[1:], F32)

        conv_a = [_CausalConv(car_a[j, :, cs], stride) for cs in cols]
        conv_g = [_CausalConv(car_g[j, :, cs], stride) for cs in cols]

    if not fused_ln:
        acc = rest[5] if sample else rest[3]

        @pl.when(j == 0)
        def _():
            acc[...] = jnp.zeros(acc.shape, F32)

    units = [(r, c) for r in range(x_ref.shape[0] // FF_ROWS) for c in range(kc)]
    partial = {}

    def up(i):
        r, c = units[i]
        xb = x_ref[r * FF_ROWS:(r + 1) * FF_ROWS, :].astype(BF16)
        return _dot(xb, wua_ref[:, cols[c]]), _dot(xb, wug_ref[:, cols[c]])

    def down(i, h):
        r, c = units[i]
        ha, hg = h
        rows = slice(r * FF_ROWS, (r + 1) * FF_ROWS)
        ya = conv_a[c](ha, wca_ref.at[:, cols[c]]) + ba_ref[:, cols[c]]
        yg = conv_g[c](hg, wcg_ref.at[:, cols[c]]) + bg_ref[:, cols[c]]
        act = (yg * jax.nn.sigmoid(yg) * ya).astype(BF16)
        total = _dot(act, wd_ref[cols[c], :])
        if c > 0:
            total = partial.pop(r) + total
        if c < kc - 1:
            partial[r] = total
        elif fused_ln:
            y_ref[rows, :] = _layer_norm(DEEPNORM_ALPHA * x_ref[rows, :] + total, g_ref[...], b_ref[...])
        else:
            acc[rows, :] = acc[rows, :] + total

    _software_pipeline(len(units), UP_AHEAD, up, down)

    for c in range(kc):
        la_ref[:, cols[c]] = conv_a[c].tail()
        lg_ref[:, cols[c]] = conv_g[c].tail()
        if not sample:
            car_a[j, :, cols[c]] = conv_a[c].tail()
            car_g[j, :, cols[c]] = conv_g[c].tail()

    if not fused_ln:
        @pl.when(j == pl.num_programs(1) - 1)
        def _():
            y_ref[...] = _layer_norm(DEEPNORM_ALPHA * x_ref[...] + acc[...], g_ref[...], b_ref[...])


def _conv_ffn(x, layer, w, state=None, seq_len=None, stride=1):
    t = x.shape[0]
    sample = state is not None
    tm = t if sample else FF_TM
    n_tiles = t // tm
    halo_rows = (CONV_W - 1) * stride if sample else HALO_PROMPT
    kc = 1 if sample else N_FF_CHUNKS
    nj = N_FF_CHUNKS // kc
    width = kc * FF_CHUNK
    resident = dict(pipeline_mode=pl.Buffered(1)) if nj == 1 else {}
    in_specs = [
        pl.BlockSpec((tm, D_MODEL), lambda i, j: (i, 0)),
        pl.BlockSpec((None, D_MODEL, width), lambda i, j: (layer, 0, j), **resident),
        pl.BlockSpec((None, D_MODEL, width), lambda i, j: (layer, 0, nj + j), **resident),
        pl.BlockSpec((None, CONV_W, width), lambda i, j: (layer, 0, j)),
        pl.BlockSpec((None, CONV_W, width), lambda i, j: (layer, 0, nj + j)),
        pl.BlockSpec((None, 1, width), lambda i, j: (layer, 0, j)),
        pl.BlockSpec((None, 1, width), lambda i, j: (layer, 0, nj + j)),
        pl.BlockSpec((None, width, D_MODEL), lambda i, j: (layer, j, 0), **resident),
        pl.BlockSpec((None, 1, D_MODEL), lambda i, j: (2 * layer + 1, 0, 0)),
        pl.BlockSpec((None, 1, D_MODEL), lambda i, j: (2 * layer + 1, 0, 0)),
    ]
    args = [x, w['w_up'], w['w_up'], w['w_fconv'], w['w_fconv'], w['b_fconv'], w['b_fconv'],
            w['w_down'], w['ln_g'], w['ln_b']]
    scratch = [] if nj == 1 else [pltpu.VMEM((tm, D_MODEL), F32)]
    if sample:
        in_specs += [pl.BlockSpec((halo_rows, width), lambda i, j: (0, j)),
                     pl.BlockSpec((halo_rows, width), lambda i, j: (0, nj + j))]
        args += [state, state]
    else:
        scratch += [pltpu.VMEM((nj, halo_rows, width), F32), pltpu.VMEM((nj, halo_rows, width), F32)]
    out_shape = [jax.ShapeDtypeStruct((t, D_MODEL), F32),
                 jax.ShapeDtypeStruct((n_tiles, halo_rows, D_FF), F32),
                 jax.ShapeDtypeStruct((n_tiles, halo_rows, D_FF), F32)]
    out_specs = [pl.BlockSpec((tm, D_MODEL), lambda i, j: (i, 0)),
                 pl.BlockSpec((None, halo_rows, width), lambda i, j: (i, 0, j)),
                 pl.BlockSpec((None, halo_rows, width), lambda i, j: (i, 0, j))]
    return pl.pallas_call(
        functools.partial(_ffn_kernel, None if sample else seq_len // tm, stride, kc),
        grid=(n_tiles, nj), in_specs=in_specs, out_specs=out_specs, out_shape=out_shape,
        scratch_shapes=scratch,
        compiler_params=_cparams(("arbitrary", "arbitrary")), name="conv_ffn",
    )(*args)


def _sconv_kernel(tiles_per_seq, stride, x_ref, win_ref, wc_ref, wout_ref, g_ref, b_ref, *rest):
    sample = tiles_per_seq is None
    if sample:
        s_ref, y_ref, last_ref = rest
        conv = _CausalConv(_state_blocks(s_ref, stride), stride)
    else:
        y_ref, last_ref, car = rest

        @pl.when((pl.program_id(0) % tiles_per_seq) == 0)
        def _():
            car[...] = jnp.zeros(car.shape, F32)

        conv = _CausalConv(car[...], stride)

    def proj_in(r):
        return _dot(x_ref[r * ROW_BLOCK:(r + 1) * ROW_BLOCK, :].astype(BF16), win_ref[...])

    def mix_out(r, bch):
        rows = slice(r * ROW_BLOCK, (r + 1) * ROW_BLOCK)
        g_b = bch[:, :D_MODEL]
        u = bch[:, D_MODEL:2 * D_MODEL] * bch[:, 2 * D_MODEL:]
        mix = _dot((g_b * conv(u, wc_ref)).astype(BF16), wout_ref[...])
        y_ref[rows, :] = _layer_norm(DEEPNORM_ALPHA * x_ref[rows, :] + mix, g_ref[...], b_ref[...])

    _software_pipeline(x_ref.shape[0] // ROW_BLOCK, 1, proj_in, mix_out)

    last_ref[...] = conv.tail()
    if not sample:
        car[...] = conv.tail()


def _short_conv(x, w, state=None, tiles_per_seq=None, stride=1):
    t = x.shape[0]
    n_tiles = t // TM
    sample = state is not None
    halo_rows = (CONV_W - 1) * stride if sample else HALO_PROMPT
    const = lambda i: (0, 0)
    row = lambda i: (i, 0)
    in_specs = [pl.BlockSpec((TM, D_MODEL), row), pl.BlockSpec(w['w_b_in'].shape, const),
                pl.BlockSpec((CONV_W, D_MODEL), const), pl.BlockSpec(w['w_b_out'].shape, const),
                pl.BlockSpec((None, 1, D_MODEL), lambda i: (2, 0, 0)),
                pl.BlockSpec((None, 1, D_MODEL), lambda i: (2, 0, 0))]
    args = [x, w['w_b_in'], w['w_b_conv'], w['w_b_out'], w['ln_g'], w['ln_b']]
    scratch = []
    if sample:
        in_specs += [pl.BlockSpec((halo_rows, D_MODEL), const)]
        args += [state]
    else:
        scratch += [pltpu.VMEM((halo_rows, D_MODEL), F32)]
    return pl.pallas_call(
        functools.partial(_sconv_kernel, None if sample else tiles_per_seq, stride),
        grid=(n_tiles,), in_specs=in_specs,
        out_specs=[pl.BlockSpec((TM, D_MODEL), row),
                   pl.BlockSpec((None, halo_rows, D_MODEL), lambda i: (i, 0, 0))],
        out_shape=[jax.ShapeDtypeStruct((t, D_MODEL), F32),
                   jax.ShapeDtypeStruct((n_tiles, halo_rows, D_MODEL), F32)],
        scratch_shapes=scratch,
        compiler_params=_cparams(("arbitrary",)), name="short_conv",
    )(*args)


def _rot_cols(w):
    return jnp.concatenate([-w[..., HALF_ROPE:], w[..., :HALF_ROPE]], axis=-1)


def _prepare_weights(w_a_down, g_q_norm, g_kv_norm, w_a_uq, w_a_uk, w_a_uv, w_a_out,
                     w_b_in, w_b_conv, w_b_out, w_f_up, w_f_conv, b_f_conv, w_f_down, ln_g, ln_b):
    o = Q_LORA + KV_LORA
    w_kr = w_a_down[:, o:]
    zpad = jnp.zeros((D_MODEL, LANES - QK_ROPE), F32)
    wd = jnp.concatenate([w_a_down[:, :o], w_kr, zpad, _rot_cols(w_kr), zpad], axis=1)

    wq3 = w_a_uq.reshape(Q_LORA, N_HEADS, QK_HEAD)
    zq = jnp.zeros((Q_LORA, N_HEADS, HEAD_PAD - QK_HEAD), F32)
    q_main = jnp.concatenate([wq3, zq], axis=-1)
    q_rot = jnp.concatenate([jnp.zeros((Q_LORA, N_HEADS, QK_NOPE), F32), _rot_cols(wq3[..., QK_NOPE:]), zq],
                            axis=-1)
    wq = jnp.concatenate([q_main.reshape(Q_LORA, -1), q_rot.reshape(Q_LORA, -1)], axis=1)

    wk = jnp.concatenate([w_a_uk, jnp.zeros((KV_LORA, N_HEADS, HEAD_PAD - QK_NOPE), F32)], axis=-1)
    wk = wk.reshape(KV_LORA, N_HEADS * HEAD_PAD)
    src = jnp.arange(LANES)[:, None]
    dst = jnp.arange(N_HEADS * HEAD_PAD)[None, :]
    e = ((src < QK_ROPE) & ((dst % HEAD_PAD) == QK_NOPE + src)).astype(BF16)

    w_ukt = jnp.transpose(w_a_uk, (1, 2, 0))
    w_ukt = jnp.concatenate([w_ukt, jnp.zeros((N_HEADS, HEAD_PAD - QK_NOPE, KV_LORA), F32)], axis=1)
    w_uvh = jnp.transpose(w_a_uv, (1, 0, 2))
    zv = jnp.zeros_like(w_uvh)
    even = (jnp.arange(N_HEADS) % 2 == 0)[:, None, None]
    w_uv_pair = jnp.where(even, jnp.concatenate([w_uvh, zv], -1), jnp.concatenate([zv, w_uvh], -1))

    return dict(
        wd=wd.astype(BF16), gq=g_q_norm.reshape(1, -1), gkv=g_kv_norm.reshape(1, -1), wq=wq.astype(BF16),
        wk=wk.T.astype(BF16), e=e.T, wv=w_a_uv.reshape(KV_LORA, -1).astype(BF16),
        w_ukt=w_ukt.astype(BF16), w_uv_pair=w_uv_pair.astype(BF16), w_a_out=w_a_out.astype(BF16),
        w_b_in=w_b_in.astype(BF16), w_b_conv=w_b_conv, w_b_out=w_b_out.astype(BF16),
        w_up=w_f_up.astype(BF16), w_fconv=w_f_conv, b_fconv=b_f_conv.reshape(DEPTH, 1, 2 * D_FF),
        w_down=w_f_down.astype(BF16), ln_g=ln_g.reshape(2 * DEPTH, 1, D_MODEL),
        ln_b=ln_b.reshape(2 * DEPTH, 1, D_MODEL),
    )


def _rope_tables(pos):
    inv = 1.0 / (ROPE_THETA ** (jnp.arange(0, QK_ROPE, 2, dtype=F32) / QK_ROPE))
    ang = pos.astype(F32)[:, None] * inv[None, :]
    cos, sin = jnp.cos(ang), jnp.sin(ang)
    p = pos.shape[0]
    one = jnp.ones((p, QK_NOPE), F32)
    z64 = jnp.zeros((p, QK_NOPE), F32)
    z32 = jnp.zeros((p, HEAD_PAD - QK_HEAD), F32)
    z96 = jnp.zeros((p, LANES - QK_ROPE), F32)
    cosg = jnp.concatenate([one, cos, cos, z32], axis=1)
    sing = jnp.concatenate([z64, sin, sin, z32], axis=1)
    ckr = jnp.concatenate([cos, cos, z96], axis=1)
    skr = jnp.concatenate([sin, sin, z96], axis=1)
    return cosg, sing, ckr, skr


def _prompt_trunk(x_prompt, w):
    b, s, _ = x_prompt.shape
    x = x_prompt.reshape(b * s, D_MODEL)
    tps = s // TM
    tabs = _rope_tables(jnp.arange(s))
    q, c_kv, k_r, k, v = _mla_proj(x, tabs, w, True, tps)
    o = _flash_attention(q, k, v, b, s)
    x = _out_proj_ln(x, o, w['w_a_out'], w['ln_g'][0], w['ln_b'][0])
    x, la0, lg0 = _conv_ffn(x, 0, w, seq_len=s)
    x, lc = _short_conv(x, w, tiles_per_seq=tps)
    x, la1, lg1 = _conv_ffn(x, 1, w, seq_len=s)

    def seq_last(a, tiles=tps):
        return a[tiles - 1::tiles, HALO_PROMPT - (CONV_W - 1):, :]

    ffn_last = lambda a: seq_last(a, s // FF_TM)
    ffn_state = jnp.stack([jnp.concatenate([ffn_last(la0), ffn_last(lg0)], axis=-1),
                           jnp.concatenate([ffn_last(la1), ffn_last(lg1)], axis=-1)])
    return (x.reshape(b, s, D_MODEL),
            c_kv.reshape(b, s // PAGE_SIZE, PAGE_SIZE, KV_LORA),
            k_r.reshape(b, s // PAGE_SIZE, PAGE_SIZE, QK_ROPE),
            seq_last(lc), ffn_state)


def _sample_trunk(x_sample, cache_c, cache_r, state_conv_b, state_ffn_conv, page_table, w):
    b, t, _ = x_sample.shape
    n = b * t
    past_len = page_table.shape[1] * cache_c.shape[1]
    tm_rows = lambda a: jnp.swapaxes(a, 0, 1).reshape(a.shape[1] * a.shape[0], a.shape[-1])
    bm_rows = lambda a, k: jnp.swapaxes(a.reshape(k, b, a.shape[-1]), 0, 1)
    x = tm_rows(x_sample)
    pos = past_len + jnp.repeat(jnp.arange(t), b)
    tabs = _rope_tables(pos)
    q, c_kv, k_r = _mla_proj(x, tabs, w, False, n // TM)

    q_lat = _absorb_q(q, w['w_ukt'])
    q_lat = jnp.transpose(q_lat.reshape(N_HEADS, t, b, KV_LORA), (2, 1, 0, 3)).reshape(b, t * N_HEADS, KV_LORA)
    q_rope = q.reshape(t, b, N_HEADS, HEAD_PAD)[..., QK_NOPE:QK_HEAD]
    q_rope = jnp.transpose(q_rope, (1, 0, 2, 3)).reshape(b, t * N_HEADS, QK_ROPE)
    pad_new = lambda a: jnp.pad(bm_rows(a, t), ((0, 0), (0, SUBLANES - t), (0, 0)))
    cache_rt = jnp.swapaxes(cache_r, 1, 2)
    o_lat = _decode_attention(q_lat, q_rope, pad_new(c_kv), pad_new(k_r), cache_c, cache_rt, page_table, t)
    o_lat = jnp.transpose(o_lat.reshape(b, t, N_HEADS, KV_LORA), (2, 1, 0, 3)).reshape(N_HEADS, n, KV_LORA)
    o = _unabsorb_o(o_lat.astype(BF16), w['w_uv_pair'])

    x = _out_proj_ln(x, o, w['w_a_out'], w['ln_g'][0], w['ln_b'][0])
    x, la0, lg0 = _conv_ffn(x, 0, w, state=tm_rows(state_ffn_conv[0]), stride=b)
    x, lc = _short_conv(x, w, state=tm_rows(state_conv_b), stride=b)
    x, la1, lg1 = _conv_ffn(x, 1, w, state=tm_rows(state_ffn_conv[1]), stride=b)

    k = CONV_W - 1
    ffn_state = jnp.stack([bm_rows(jnp.concatenate([la0[0], lg0[0]], axis=-1), k),
                           bm_rows(jnp.concatenate([la1[0], lg1[0]], axis=-1), k)])
    return (bm_rows(x, t), bm_rows(c_kv, t), bm_rows(k_r, t), bm_rows(lc[0], k), ffn_state)


def kernel(x_prompt, x_sample, cache_kv_latent, cache_k_rope, state_conv_b, state_ffn_conv, page_table,
           w_a_down, g_q_norm, g_kv_norm, w_a_uq, w_a_uk, w_a_uv, w_a_out,
           w_b_in, w_b_conv, w_b_out, w_f_up, w_f_conv, b_f_conv, w_f_down, ln_g, ln_b):
    w = _prepare_weights(w_a_down, g_q_norm, g_kv_norm, w_a_uq, w_a_uk, w_a_uv, w_a_out,
                         w_b_in, w_b_conv, w_b_out, w_f_up, w_f_conv, b_f_conv, w_f_down, ln_g, ln_b)
    y_p, lat_p, rope_p, conv_p, ffn_p = _prompt_trunk(x_prompt, w)
    y_s, lat_s, rope_s, conv_s, ffn_s = _sample_trunk(
        x_sample, cache_kv_latent, cache_k_rope, state_conv_b, state_ffn_conv, page_table, w)
    return (y_p, y_s, lat_p, rope_p, lat_s, rope_s, conv_p, conv_s, ffn_p, ffn_s)
```
